```python
import jax, jax.numpy as jnp
from jax import lax
import numpy as np

D_MODEL = 1024
BATCH = 2
SEQ = 8192
DEPTH = 1

EPS = 1e-6
D_FF = 2816
N_MOD = 9
GLA_HEADS = 4
GLA_DK = D_MODEL // 2
GLA_DV = D_MODEL
GLA_HK = GLA_DK // GLA_HEADS
GLA_HV = GLA_DV // GLA_HEADS
GLA_RANK = 16
GLA_TAU = 16.0
GLA_CHUNK = 64
FOX_HEADS = 16
FOX_HD = D_MODEL // FOX_HEADS
FOX_W = FOX_HEADS * FOX_HD
Q_BLOCK = 128
SPLITS = (GLA_DK, GLA_DK, GLA_DV, GLA_RANK, GLA_DV, FOX_W, FOX_W, FOX_W, FOX_HEADS, D_MODEL, D_MODEL)
IN_COLS = 2 * GLA_DK + 2 * GLA_DV + GLA_RANK + 3 * FOX_W + FOX_HEADS + 2 * D_MODEL

kernel_name = "hybrid_gla_fox_macaron_adaln"


def rmsnorm(x, g):
    xf = x.astype(jnp.float32)
    y = xf * lax.rsqrt(jnp.mean(xf * xf, axis=-1, keepdims=True) + EPS)
    return (y * g.astype(jnp.float32)).astype(x.dtype)


def swiglu(h, w_gu, w_dn):
    g, u = jnp.split(h @ w_gu, 2, axis=-1)
    return (jax.nn.silu(g) * u) @ w_dn


def gla_chunked(q, k, v, log_a):
    B, H, T, dk = q.shape
    dv = v.shape[-1]
    C = GLA_CHUNK
    N = T // C

    def to_chunks(a):
        return a.reshape(B, H, N, C, a.shape[-1]).transpose(2, 0, 1, 3, 4)

    qc, kc, vc, gc = to_chunks(q), to_chunks(k), to_chunks(v), to_chunks(log_a)
    causal = jnp.tril(jnp.ones((C, C), dtype=bool))[:, :, None]

    def step(S, inp):
        qi, ki, vi, gi = inp
        b = jnp.cumsum(gi, axis=2)
        o_inter = jnp.einsum('bhtd,bhde->bhte', qi * jnp.exp(b), S)
        diff = b[:, :, :, None, :] - b[:, :, None, :, :]
        decay = jnp.exp(jnp.where(causal, diff, -jnp.inf))
        A = jnp.einsum('bhtd,bhsd,bhtsd->bhts', qi, ki, decay)
        o = o_inter + jnp.einsum('bhts,bhse->bhte', A, vi)
        b_last = b[:, :, -1:, :]
        S_new = jnp.exp(b_last[:, :, 0, :, None]) * S + jnp.einsum(
            'bhsd,bhse->bhde', ki * jnp.exp(b_last - b), vi)
        return S_new, o

    S0 = jnp.zeros((B, H, dk, dv), jnp.float32)
    _, o = lax.scan(step, S0, (qc, kc, vc, gc))
    return o.transpose(1, 2, 0, 3, 4).reshape(B, H, T, dv)


def fox_attention(q, k, v, logf):
    B, H, T, hd = q.shape
    F = jnp.cumsum(logf, axis=-1)
    nb = T // Q_BLOCK
    qb = q.reshape(B, H, nb, Q_BLOCK, hd).transpose(2, 0, 1, 3, 4)
    Fb = F.reshape(B, H, nb, Q_BLOCK).transpose(2, 0, 1, 3)
    kpos = jnp.arange(T)
    scale = hd ** -0.5

    def block(args):
        qi, Fi, i = args
        s = jnp.einsum('bhqd,bhkd->bhqk', qi, k).astype(jnp.float32) * scale
        s = s + Fi[..., None] - F[:, :, None, :]
        qpos = i * Q_BLOCK + jnp.arange(Q_BLOCK)
        s = jnp.where(kpos[None, :] <= qpos[:, None], s, -jnp.inf)
        p = jax.nn.softmax(s, axis=-1)
        return jnp.einsum('bhqk,bhkd->bhqd', p.astype(v.dtype), v)

    o = lax.map(block, (qb, Fb, jnp.arange(nb)))
    return o.transpose(1, 2, 0, 3, 4).reshape(B, H, T, hd)


def token_mixer(h, w_in, w_a2, b_a, b_f, g_gla, w_pa, w_pb, w_out):
    B, T, _ = h.shape
    f32 = jnp.float32
    z = h @ w_in
    cuts = np.cumsum(SPLITS)[:-1].tolist()
    q_a, k_a, v_a, a_low, r_a, q_b, k_b, v_b, f_b, gate_a, gate_b = jnp.split(z, cuts, axis=-1)

    def heads(t, n):
        return t.reshape(B, T, n, -1).transpose(0, 2, 1, 3)

    log_a = jax.nn.log_sigmoid((a_low @ w_a2 + b_a).astype(f32)) / GLA_TAU
    o_a = gla_chunked(heads(q_a.astype(f32) * (GLA_HK ** -0.5), GLA_HEADS),
                      heads(k_a.astype(f32), GLA_HEADS),
                      heads(v_a.astype(f32), GLA_HEADS),
                      heads(log_a, GLA_HEADS))
    o_a = rmsnorm(o_a.transpose(0, 2, 1, 3), g_gla.reshape(GLA_HEADS, GLA_HV))
    o_a = o_a.astype(h.dtype).reshape(B, T, GLA_DV) * jax.nn.silu(r_a)

    logf = jax.nn.log_sigmoid((f_b + b_f).astype(f32)).transpose(0, 2, 1)
    o_b = fox_attention(heads(q_b, FOX_HEADS), heads(k_b, FOX_HEADS), heads(v_b, FOX_HEADS), logf)
    o_b = o_b.transpose(0, 2, 1, 3).reshape(B, T, FOX_W)

    merged = jax.nn.sigmoid(gate_a) * (o_a @ w_pa) + jax.nn.sigmoid(gate_b) * (o_b @ w_pb)
    return merged @ w_out


def setup_inputs(seed: int = 0) -> dict:
    key = jax.random.key(seed)
    ks = jax.random.split(key, 20)
    L, D = DEPTH, D_MODEL
    nrm = jax.random.normal

    def w(k, shape, fan_in):
        return nrm(k, shape, jnp.float32) * fan_in ** -0.5

    return {
        "x": nrm(ks[0], (BATCH, SEQ, D), jnp.float32),
        "c": nrm(ks[1], (BATCH, D), jnp.float32),
        "w_ada": w(ks[2], (L, D, N_MOD * D), D) * 0.5,
        "b_ada": 0.02 * nrm(ks[3], (L, N_MOD * D), jnp.float32),
        "g_pre": 1.0 + 0.05 * nrm(ks[4], (L, 3, D), jnp.float32),
        "g_post": 1.0 + 0.05 * nrm(ks[5], (L, 3, D), jnp.float32),
        "w_gu1": w(ks[6], (L, D, 2 * D_FF), D),
        "w_dn1": w(ks[7], (L, D_FF, D), D_FF),
        "w_gu2": w(ks[8], (L, D, 2 * D_FF), D),
        "w_dn2": w(ks[9], (L, D_FF, D), D_FF),
        "w_in": w(ks[10], (L, D, IN_COLS), D),
        "w_a2": w(ks[11], (L, GLA_RANK, GLA_DK), GLA_RANK),
        "b_a": 0.1 * nrm(ks[12], (L, GLA_DK), jnp.float32),
        "b_f": 1.0 + 0.1 * nrm(ks[13], (L, FOX_HEADS), jnp.float32),
        "g_gla": 1.0 + 0.05 * nrm(ks[14], (L, GLA_DV), jnp.float32),
        "w_pa": w(ks[15], (L, GLA_DV, D), GLA_DV),
        "w_pb": w(ks[16], (L, FOX_W, D), FOX_W),
        "w_out": w(ks[17], (L, D, D), D),
    }


def reference(x, c, w_ada, b_ada, g_pre, g_post, w_gu1, w_dn1, w_gu2, w_dn2,
              w_in, w_a2, b_a, b_f, g_gla, w_pa, w_pb, w_out):
    B = x.shape[0]
    for l in range(DEPTH):
        mods = (jax.nn.silu(c) @ w_ada[l] + b_ada[l]).reshape(B, N_MOD, 1, D_MODEL)
        sh1, sc1, gt1, sh2, sc2, gt2, sh3, sc3, gt3 = [mods[:, i] for i in range(N_MOD)]

        h = rmsnorm(x, g_pre[l, 0]) * (1 + sc1) + sh1
        x = x + 0.5 * gt1 * rmsnorm(swiglu(h, w_gu1[l], w_dn1[l]), g_post[l, 0])

        h = rmsnorm(x, g_pre[l, 1]) * (1 + sc2) + sh2
        y = token_mixer(h, w_in[l], w_a2[l], b_a[l], b_f[l], g_gla[l], w_pa[l], w_pb[l], w_out[l])
        x = x + gt2 * rmsnorm(y, g_post[l, 1])

        h = rmsnorm(x, g_pre[l, 2]) * (1 + sc3) + sh3
        x = x + 0.5 * gt3 * rmsnorm(swiglu(h, w_gu2[l], w_dn2[l]), g_post[l, 2])
    return x
```

```python
import functools

import numpy as np
import jax
import jax.numpy as jnp
from jax import lax
from jax.experimental import pallas as pl
from jax.experimental.pallas import tpu as pltpu

F32 = jnp.float32
BF16 = jnp.bfloat16

EPS = 1e-6
LOG2E = 1.4426950408889634

D_MODEL = 1024
D_FF = 2816
N_MOD = 9
GLA_HEADS = 4
GLA_DK = 512
GLA_DV = 1024
GLA_HK = GLA_DK // GLA_HEADS
GLA_HV = GLA_DV // GLA_HEADS
GLA_RANK = 16
GLA_TAU = 16.0
FOX_HEADS = 16
FOX_HD = 64
FOX_W = FOX_HEADS * FOX_HD
FOX_PAIRS = FOX_HEADS // 2

LANES = 128
VMEM_LIMIT = 56 * 1024 * 1024

FFN_TM = 512
FFN_CHUNKS = ((0, 1024), (1024, 1024), (2048, 768))
PROJ_TM = 256
GLA_CHUNK = 128
GLA_TT = 512
GLA_LEVELS = (64, 32, 16, 8, 4, 2, 1)
FOX_BQ = 512
MERGE_TM = 512

_C_QA, _C_KA, _C_VA, _C_RA = 0, 512, 1024, 2048
_C_QB, _C_KB, _C_VB, _C_GA, _C_GB = 3072, 4096, 5120, 6144, 7168
_C_SMALL = 8192
_W_COLS = 8320
_F_LANE0 = 16


def _resident(shape):
    nd = len(shape)
    return pl.BlockSpec(shape, lambda *_: (0,) * nd, pipeline_mode=pl.Buffered(1))


def _rms(x):
    return x * lax.rsqrt(jnp.mean(x * x, axis=-1, keepdims=True) + EPS)


def _log_sigmoid(x):
    return jnp.minimum(x, 0.0) - jnp.log(1.0 + jnp.exp(-jnp.abs(x)))


def _adaln_kernel(c_ref, w_ref, b_ref, o_ref):
    c = c_ref[...]
    s = c * jax.nn.sigmoid(c)
    o_ref[...] = jnp.dot(s.astype(BF16), w_ref[...].astype(BF16),
                         preferred_element_type=F32) + b_ref[...]


def _adaln(c_pad, w_ada, b_ada):
    rows = c_pad.shape[0]
    ncol = w_ada.shape[1]
    tn = 1024
    return pl.pallas_call(
        _adaln_kernel,
        out_shape=jax.ShapeDtypeStruct((rows, ncol), F32),
        grid=(ncol // tn,),
        in_specs=[pl.BlockSpec((rows, D_MODEL), lambda j: (0, 0)),
                  pl.BlockSpec((D_MODEL, tn), lambda j: (0, j)),
                  pl.BlockSpec((1, tn), lambda j: (0, j))],
        out_specs=pl.BlockSpec((rows, tn), lambda j: (0, j)),
        compiler_params=pltpu.CompilerParams(dimension_semantics=("arbitrary",)),
        name="adaln",
    )(c_pad, w_ada, b_ada)


def _ffn_kernel(x_ref, mod_ref, gpre_ref, gpost_ref, wgu_ref, wdn_ref, o_ref, *, mod0):
    x = x_ref[...]
    sh = mod_ref[0, mod0:mod0 + 1, :]
    sc = mod_ref[0, mod0 + 1:mod0 + 2, :]
    gt = mod_ref[0, mod0 + 2:mod0 + 3, :]
    h = (_rms(x) * gpre_ref[...]) * (1.0 + sc) + sh
    hb = h.astype(BF16)
    acc = None
    for c0, cw in FFN_CHUNKS:
        g = jnp.dot(hb, wgu_ref[:, c0:c0 + cw], preferred_element_type=F32)
        u = jnp.dot(hb, wgu_ref[:, D_FF + c0:D_FF + c0 + cw], preferred_element_type=F32)
        a = (g * jax.nn.sigmoid(g) * u).astype(BF16)
        part = jnp.dot(a, wdn_ref[c0:c0 + cw, :], preferred_element_type=F32)
        acc = part if acc is None else acc + part
    o_ref[...] = x + (0.5 * gt) * (_rms(acc) * gpost_ref[...])


def _ffn(x2d, mods, g_pre, g_post, w_gu, w_dn, *, mod0, seq):
    n = x2d.shape[0]
    tiles_per_batch = seq // FFN_TM
    return pl.pallas_call(
        functools.partial(_ffn_kernel, mod0=mod0),
        out_shape=jax.ShapeDtypeStruct((n, D_MODEL), F32),
        grid=(n // FFN_TM,),
        in_specs=[pl.BlockSpec((FFN_TM, D_MODEL), lambda i: (i, 0)),
                  pl.BlockSpec((1, N_MOD, D_MODEL), lambda i: (i // tiles_per_batch, 0, 0)),
                  _resident((1, D_MODEL)),
                  _resident((1, D_MODEL)),
                  _resident((D_MODEL, 2 * D_FF)),
                  _resident((D_FF, D_MODEL))],
        out_specs=pl.BlockSpec((FFN_TM, D_MODEL), lambda i: (i, 0)),
        compiler_params=pltpu.CompilerParams(dimension_semantics=("arbitrary",),
                                             vmem_limit_bytes=VMEM_LIMIT),
        name="ffn",
    )(x2d, mods, g_pre, g_post, w_gu, w_dn)


def _proj_kernel(x_ref, mod_ref, gpre_ref, w_ref, wa2_ref, ba_ref, bsm_ref, eqk_ref,
                 qa_ref, ka_ref, va_ref, la_ref, ra_ref,
                 qe_ref, qo_ref, ke_ref, ko_ref, vb_ref, sga_ref, sgb_ref,
                 fcar_ref, *, tiles_per_batch):
    i = pl.program_id(0)
    tm = x_ref.shape[0]

    @pl.when(i % tiles_per_batch == 0)
    def _():
        fcar_ref[...] = jnp.zeros_like(fcar_ref)

    x = x_ref[...]
    sh = mod_ref[0, 3:4, :]
    sc = mod_ref[0, 4:5, :]
    hb = ((_rms(x) * gpre_ref[...]) * (1.0 + sc) + sh).astype(BF16)

    def proj(c0, cw):
        return jnp.dot(hb, w_ref[:, c0:c0 + cw], preferred_element_type=F32)

    qa_ref[...] = proj(_C_QA, GLA_DK) * (GLA_HK ** -0.5)
    ka_ref[...] = proj(_C_KA, GLA_DK)
    va_ref[...] = proj(_C_VA, GLA_DV).astype(BF16)
    r = proj(_C_RA, GLA_DV)
    ra_ref[...] = (r * jax.nn.sigmoid(r)).astype(BF16)

    zs = proj(_C_SMALL, LANES)
    xa = jnp.dot(zs.astype(BF16), wa2_ref[...], preferred_element_type=F32) + ba_ref[...]
    la_ref[...] = _log_sigmoid(xa) * (1.0 / GLA_TAU)

    lane = lax.broadcasted_iota(jnp.int32, (tm, LANES), 1)
    row = lax.broadcasted_iota(jnp.int32, (tm, LANES), 0)
    in_f = (lane >= _F_LANE0) & (lane < _F_LANE0 + FOX_HEADS)
    f = jnp.where(in_f, _log_sigmoid(zs + bsm_ref[...]), 0.0)
    shift = 1
    while shift < tm:
        f = f + jnp.where(row >= shift, pltpu.roll(f, shift, 0), 0.0)
        shift *= 2
    f = f + fcar_ref[...]
    fcar_ref[...] = f[tm - 1:tm, :]

    f2 = f * LOG2E
    p0 = f2.astype(BF16).astype(F32)
    r1 = f2 - p0
    p1 = r1.astype(BF16).astype(F32)
    p2 = (r1 - p1).astype(BF16).astype(F32)
    fc = p0 + pltpu.roll(p1, FOX_HEADS, 1) + pltpu.roll(p2, 2 * FOX_HEADS, 1)
    fc = jnp.where(lane == 0, 1.0, fc)
    aug = jnp.dot(fc.astype(BF16), eqk_ref[...], preferred_element_type=F32)
    augq = aug[:, :FOX_W]
    augk = aug[:, FOX_W:]

    lane_w = lax.broadcasted_iota(jnp.int32, (tm, FOX_W), 1)
    low_half = (lane_w & (LANES - 1)) < FOX_HD
    zq = proj(_C_QB, FOX_W) * (FOX_HD ** -0.5 * LOG2E)
    qe_ref[...] = jnp.where(low_half, zq, augq).astype(BF16)
    qo_ref[...] = jnp.where(low_half, augq, zq).astype(BF16)
    zk = proj(_C_KB, FOX_W)
    ke_ref[...] = jnp.where(low_half, zk, augk).astype(BF16)
    ko_ref[...] = jnp.where(low_half, augk, zk).astype(BF16)
    vb_ref[...] = proj(_C_VB, FOX_W).astype(BF16)
    sga_ref[...] = jax.nn.sigmoid(proj(_C_GA, D_MODEL)).astype(BF16)
    sgb_ref[...] = jax.nn.sigmoid(proj(_C_GB, D_MODEL)).astype(BF16)


def _proj(x2d, mods, g_pre, w_all, w_a2p, b_a, b_small, e_qk, *, seq):
    n = x2d.shape[0]
    tm = PROJ_TM
    tiles_per_batch = seq // tm
    row_spec = lambda w: pl.BlockSpec((tm, w), lambda i: (i, 0))
    out_shapes = [
        jax.ShapeDtypeStruct((n, GLA_DK), F32),
        jax.ShapeDtypeStruct((n, GLA_DK), F32),
        jax.ShapeDtypeStruct((n, GLA_DV), BF16),
        jax.ShapeDtypeStruct((n, GLA_DK), F32),
        jax.ShapeDtypeStruct((n, GLA_DV), BF16),
        jax.ShapeDtypeStruct((n, FOX_W), BF16),
        jax.ShapeDtypeStruct((n, FOX_W), BF16),
        jax.ShapeDtypeStruct((n, FOX_W), BF16),
        jax.ShapeDtypeStruct((n, FOX_W), BF16),
        jax.ShapeDtypeStruct((n, FOX_W), BF16),
        jax.ShapeDtypeStruct((n, D_MODEL), BF16),
        jax.ShapeDtypeStruct((n, D_MODEL), BF16),
    ]
    return pl.pallas_call(
        functools.partial(_proj_kernel, tiles_per_batch=tiles_per_batch),
        out_shape=out_shapes,
        grid=(n // tm,),
        in_specs=[row_spec(D_MODEL),
                  pl.BlockSpec((1, N_MOD, D_MODEL), lambda i: (i // tiles_per_batch, 0, 0)),
                  _resident((1, D_MODEL)),
                  _resident((D_MODEL, _W_COLS)),
                  _resident((LANES, GLA_DK)),
                  _resident((1, GLA_DK)),
                  _resident((1, LANES)),
                  _resident((LANES, 2 * FOX_W))],
        out_specs=[row_spec(s.shape[1]) for s in out_shapes],
        scratch_shapes=[pltpu.VMEM((1, LANES), F32)],
        compiler_params=pltpu.CompilerParams(dimension_semantics=("arbitrary",),
                                             vmem_limit_bytes=VMEM_LIMIT),
        name="proj",
    )(x2d, mods, g_pre, w_all, w_a2p, b_a, b_small, e_qk)


def _gla_tables():
    c = GLA_CHUNK
    t = np.arange(c)[:, None]
    u = np.arange(c)[None, :]
    blocks = [(u <= t), (u > t)]
    masks = [(t == u)]
    for s in GLA_LEVELS:
        piv = (t // (2 * s)) * (2 * s) + s
        upper = t >= piv
        blocks.append(np.where(upper, (u > piv) & (u <= t), (u > t) & (u <= piv)))
        same = (t // (2 * s)) == (u // (2 * s))
        masks.append(same & (t % (2 * s) >= s) & (u % (2 * s) < s))
    rmat = np.concatenate(blocks, axis=0).astype(np.float32)
    mask = np.stack(masks, axis=0).astype(np.float32)
    return rmat, mask


def _dot_nt(a, b):
    return lax.dot_general(a, b, (((1,), (1,)), ((), ())), preferred_element_type=F32)


def _gla_kernel(q_ref, k_ref, v_ref, g_ref, r_ref, gg_ref, rmat_ref, mask_ref, o_ref, s_ref):
    c = GLA_CHUNK

    @pl.when(pl.program_id(2) == 0)
    def _():
        s_ref[...] = jnp.zeros_like(s_ref)

    state = s_ref[...]
    for ci in range(q_ref.shape[0] // c):
        sl = pl.ds(ci * c, c)
        q = q_ref[sl, :]
        k = k_ref[sl, :]
        g = g_ref[sl, :]
        v = v_ref[sl, :]
        g_hi = g.astype(BF16)
        g_lo = (g - g_hi.astype(F32)).astype(BF16)
        x2 = jnp.dot(rmat_ref[...], jnp.concatenate([g_hi, g_lo], axis=1),
                     preferred_element_type=F32)
        xs = x2[:, :GLA_HK] + x2[:, GLA_HK:]
        eb = jnp.exp(xs[0:c])
        ebl = jnp.exp(xs[c:2 * c])

        a = mask_ref[0] * _dot_nt(q.astype(BF16), k.astype(BF16))
        for li in range(len(GLA_LEVELS)):
            f = jnp.exp(xs[(2 + li) * c:(3 + li) * c])
            a = a + mask_ref[li + 1] * _dot_nt((q * f).astype(BF16), (k * f).astype(BF16))

        lhs = jnp.concatenate([(q * eb).astype(BF16), a.astype(BF16)], axis=1)
        rhs = jnp.concatenate([state.astype(BF16), v], axis=0)
        o = jnp.dot(lhs, rhs, preferred_element_type=F32)

        upd = jnp.dot((k * ebl).T.astype(BF16), v, preferred_element_type=F32)
        decay = eb.T[:, c - 1:c]
        state = decay * state + upd

        on = _rms(o) * gg_ref[0]
        o_ref[sl, :] = (on * r_ref[sl, :].astype(F32)).astype(BF16)
    s_ref[...] = state


def _gla(qa, ka, va, la, ra, g_gla, rmat, mask, *, batch, seq):
    n = qa.shape[0]
    tiles = seq // GLA_TT
    rows = lambda b, h, t: b * tiles + t
    return pl.pallas_call(
        _gla_kernel,
        out_shape=jax.ShapeDtypeStruct((n, GLA_DV), BF16),
        grid=(batch, GLA_HEADS, tiles),
        in_specs=[pl.BlockSpec((GLA_TT, GLA_HK), lambda b, h, t: (rows(b, h, t), h)),
                  pl.BlockSpec((GLA_TT, GLA_HK), lambda b, h, t: (rows(b, h, t), h)),
                  pl.BlockSpec((GLA_TT, GLA_HV), lambda b, h, t: (rows(b, h, t), h)),
                  pl.BlockSpec((GLA_TT, GLA_HK), lambda b, h, t: (rows(b, h, t), h)),
                  pl.BlockSpec((GLA_TT, GLA_HV), lambda b, h, t: (rows(b, h, t), h)),
                  pl.BlockSpec((1, 1, GLA_HV), lambda b, h, t: (h, 0, 0)),
                  _resident(rmat.shape),
                  _resident(mask.shape)],
        out_specs=pl.BlockSpec((GLA_TT, GLA_HV), lambda b, h, t: (rows(b, h, t), h)),
        scratch_shapes=[pltpu.VMEM((GLA_HK, GLA_HV), F32)],
        compiler_params=pltpu.CompilerParams(
            dimension_semantics=("arbitrary", "arbitrary", "arbitrary"),
            vmem_limit_bytes=VMEM_LIMIT),
        name="gla",
    )(qa, ka, va, la, ra, g_gla, rmat, mask)


def _fox_kernel(qe_ref, qo_ref, ke_ref, ko_ref, v_ref, o_ref, m_ref, acc_ref):
    i = pl.program_id(2)
    bq = qe_ref.shape[0]
    bk = bq
    q_refs = (qe_ref, qo_ref)
    k_refs = (ke_ref, ko_ref)

    m_ref[...] = jnp.full_like(m_ref, -jnp.inf)
    acc_ref[...] = jnp.zeros_like(acc_ref)

    def step(t, masked):
        ks = pl.ds(pl.multiple_of(t * bk, bk), bk)
        vp = v_ref[ks, :]
        vext = jnp.concatenate([vp, jnp.ones_like(vp)], axis=1)
        for hd in range(2):
            s = _dot_nt(q_refs[hd][...], k_refs[hd][ks, :])
            if masked:
                r_i = lax.broadcasted_iota(jnp.int32, (bq, bk), 0)
                c_i = lax.broadcasted_iota(jnp.int32, (bq, bk), 1)
                s = jnp.where(c_i <= r_i, s, -jnp.inf)
            m_old = m_ref[hd]
            m_new = jnp.maximum(m_old, jnp.max(s, axis=1, keepdims=True))
            p = jnp.exp2(s - m_new).astype(BF16)
            acc_ref[hd] = jnp.exp2(m_old - m_new) * acc_ref[hd] + jnp.dot(
                p, vext, preferred_element_type=F32)
            m_ref[hd] = m_new

    def body(t, carry):
        step(t, False)
        return carry

    lax.fori_loop(0, i, body, 0)
    step(i, True)

    lane = lax.broadcasted_iota(jnp.int32, (bq, LANES), 1)
    oe = acc_ref[0]
    oo = acc_ref[1]
    o_ref[...] = jnp.where(lane < FOX_HD,
                           oe[:, :LANES] / oe[:, LANES:],
                           oo[:, :LANES] / oo[:, LANES:]).astype(BF16)


def _fox(qe, qo, ke, ko, vb, *, batch, seq):
    n = qe.shape[0]
    bq = FOX_BQ
    tiles = seq // bq
    q_spec = pl.BlockSpec((bq, LANES), lambda b, j, i: (b * tiles + i, j))
    kv_spec = pl.BlockSpec((seq, LANES), lambda b, j, i: (b, j))
    return pl.pallas_call(
        _fox_kernel,
        out_shape=jax.ShapeDtypeStruct((n, FOX_W), BF16),
        grid=(batch, FOX_PAIRS, tiles),
        in_specs=[q_spec, q_spec, kv_spec, kv_spec, kv_spec],
        out_specs=pl.BlockSpec((bq, LANES), lambda b, j, i: (b * tiles + i, j)),
        scratch_shapes=[pltpu.VMEM((2, bq, 1), F32),
                        pltpu.VMEM((2, bq, 2 * LANES), F32)],
        compiler_params=pltpu.CompilerParams(
            dimension_semantics=("arbitrary", "arbitrary", "arbitrary"),
            vmem_limit_bytes=VMEM_LIMIT),
        name="fox",
    )(qe, qo, ke, ko, vb)


def _merge_kernel(x_ref, mod_ref, gpost_ref, oa_ref, ob_ref, sga_ref, sgb_ref,
                  wpa_ref, wpb_ref, wout_ref, o_ref):
    pa = jnp.dot(oa_ref[...], wpa_ref[...], preferred_element_type=F32)
    pb = jnp.dot(ob_ref[...], wpb_ref[...], preferred_element_type=F32)
    merged = sga_ref[...].astype(F32) * pa + sgb_ref[...].astype(F32) * pb
    y = jnp.dot(merged.astype(BF16), wout_ref[...], preferred_element_type=F32)
    gt = mod_ref[0, 5:6, :]
    o_ref[...] = x_ref[...] + gt * (_rms(y) * gpost_ref[...])


def _merge(x2d, mods, g_post, oa, ob, sga, sgb, w_pa, w_pb, w_out, *, seq):
    n = x2d.shape[0]
    tm = MERGE_TM
    tiles_per_batch = seq // tm
    row_spec = pl.BlockSpec((tm, D_MODEL), lambda i: (i, 0))
    return pl.pallas_call(
        _merge_kernel,
        out_shape=jax.ShapeDtypeStruct((n, D_MODEL), F32),
        grid=(n // tm,),
        in_specs=[row_spec,
                  pl.BlockSpec((1, N_MOD, D_MODEL), lambda i: (i // tiles_per_batch, 0, 0)),
                  _resident((1, D_MODEL)),
                  row_spec, row_spec, row_spec, row_spec,
                  _resident((D_MODEL, D_MODEL)),
                  _resident((D_MODEL, D_MODEL)),
                  _resident((D_MODEL, D_MODEL))],
        out_specs=row_spec,
        compiler_params=pltpu.CompilerParams(dimension_semantics=("arbitrary",),
                                             vmem_limit_bytes=VMEM_LIMIT),
        name="merge",
    )(x2d, mods, g_post, oa, ob, sga, sgb, w_pa, w_pb, w_out)


def _fox_bias_placement():
    e = np.zeros((LANES, 2 * FOX_W), np.float32)
    for h in range(FOX_HEADS):
        base = LANES * (h // 2) + (FOX_HD if h % 2 == 0 else 0)
        for p in range(3):
            src = _F_LANE0 + FOX_HEADS * p + h
            e[src, base + p] = 1.0
            e[0, base + 3 + p] = 1.0
            e[0, FOX_W + base + p] = 1.0
            e[src, FOX_W + base + 3 + p] = -1.0
    return e


def kernel(x, c, w_ada, b_ada, g_pre, g_post, w_gu1, w_dn1, w_gu2, w_dn2,
           w_in, w_a2, b_a, b_f, g_gla, w_pa, w_pb, w_out):
    batch, seq, d = x.shape
    n = batch * seq
    depth = w_ada.shape[0]
    rmat_np, mask_np = _gla_tables()
    rmat = jnp.asarray(rmat_np, BF16)
    mask = jnp.asarray(mask_np, F32)
    e_qk = jnp.asarray(_fox_bias_placement(), BF16)

    x2d = x.reshape(n, d)
    c_pad = jnp.pad(c, ((0, 16 - batch), (0, 0)))
    for l in range(depth):
        mods = _adaln(c_pad, w_ada[l], b_ada[l][None, :])[:batch].reshape(batch, N_MOD, d)

        x2d = _ffn(x2d, mods, g_pre[l, 0][None, :], g_post[l, 0][None, :],
                   w_gu1[l].astype(BF16), w_dn1[l].astype(BF16), mod0=0, seq=seq)

        wi = w_in[l]
        cuts = np.cumsum((GLA_DK, GLA_DK, GLA_DV, GLA_RANK, GLA_DV, FOX_W, FOX_W, FOX_W,
                          FOX_HEADS, D_MODEL, D_MODEL))[:-1].tolist()
        (w_qa, w_ka, w_va, w_al, w_ra, w_qb, w_kb, w_vb, w_fb, w_ga, w_gb) = jnp.split(wi, cuts, axis=1)
        w_small = jnp.concatenate(
            [w_al, w_fb, jnp.zeros((d, LANES - GLA_RANK - FOX_HEADS), wi.dtype)], axis=1)
        w_all = jnp.concatenate(
            [w_qa, w_ka, w_va, w_ra, w_qb, w_kb, w_vb, w_ga, w_gb, w_small], axis=1).astype(BF16)
        w_a2p = jnp.pad(w_a2[l], ((0, LANES - GLA_RANK), (0, 0))).astype(BF16)
        b_small = jnp.pad(b_f[l], (_F_LANE0, LANES - _F_LANE0 - FOX_HEADS))[None, :]

        (qa, ka, va, la, ra, qe, qo, ke, ko, vb, sga, sgb) = _proj(
            x2d, mods, g_pre[l, 1][None, :], w_all, w_a2p, b_a[l][None, :], b_small, e_qk, seq=seq)

        oa = _gla(qa, ka, va, la, ra, g_gla[l].reshape(GLA_HEADS, 1, GLA_HV), rmat, mask,
                  batch=batch, seq=seq)
        ob = _fox(qe, qo, ke, ko, vb, batch=batch, seq=seq)

        x2d = _merge(x2d, mods, g_post[l, 1][None, :], oa, ob, sga, sgb,
                     w_pa[l].astype(BF16), w_pb[l].astype(BF16), w_out[l].astype(BF16), seq=seq)

        x2d = _ffn(x2d, mods, g_pre[l, 2][None, :], g_post[l, 2][None, :],
                   w_gu2[l].astype(BF16), w_dn2[l].astype(BF16), mod0=6, seq=seq)
    return x2d.reshape(batch, seq, d)
```

```python
import functools

import numpy as np
import jax
import jax.numpy as jnp
from jax import lax
from jax.experimental import pallas as pl
from jax.experimental.pallas import tpu as pltpu

F32 = jnp.float32
BF16 = jnp.bfloat16

EPS = 1e-6
LOG2E = 1.4426950408889634

D_MODEL = 1024
D_FF = 2816
N_MOD = 9
GLA_HEADS = 4
GLA_DK = 512
GLA_DV = 1024
GLA_HK = GLA_DK // GLA_HEADS
GLA_HV = GLA_DV // GLA_HEADS
GLA_RANK = 16
GLA_TAU = 16.0
FOX_HEADS = 16
FOX_HD = 64
FOX_W = FOX_HEADS * FOX_HD
FOX_PAIRS = FOX_HEADS // 2

LANES = 128
VMEM_LIMIT = 56 * 1024 * 1024

FFN_TM = 512
FFN_CHUNKS = ((0, 1024), (1024, 1024), (2048, 768))
PROJ_TM = 256
GLA_CHUNK = 128
GLA_TT = 512
GLA_LEVELS = (64, 32, 16, 8, 4, 2, 1)
FOX_BQ = 512
MERGE_TM = 512

_C_QA, _C_KA, _C_VA, _C_RA = 0, 512, 1024, 2048
_C_QB, _C_KB, _C_VB, _C_GA, _C_GB = 3072, 4096, 5120, 6144, 7168
_C_SMALL = 8192
_W_COLS = 8320
_F_LANE0 = 16


def _resident(shape):
    nd = len(shape)
    return pl.BlockSpec(shape, lambda *_: (0,) * nd, pipeline_mode=pl.Buffered(1))


def _rms(x):
    return x * lax.rsqrt(jnp.mean(x * x, axis=-1, keepdims=True) + EPS)


def _log_sigmoid(x):
    return jnp.minimum(x, 0.0) - jnp.log(1.0 + jnp.exp(-jnp.abs(x)))


def _adaln_kernel(c_ref, w_ref, b_ref, o_ref):
    c = c_ref[...]
    s = c * jax.nn.sigmoid(c)
    o_ref[...] = jnp.dot(s.astype(BF16), w_ref[...].astype(BF16),
                         preferred_element_type=F32) + b_ref[...]


def _adaln(c_pad, w_ada, b_ada):
    rows = c_pad.shape[0]
    ncol = w_ada.shape[1]
    tn = 1024
    return pl.pallas_call(
        _adaln_kernel,
        out_shape=jax.ShapeDtypeStruct((rows, ncol), F32),
        grid=(ncol // tn,),
        in_specs=[pl.BlockSpec((rows, D_MODEL), lambda j: (0, 0)),
                  pl.BlockSpec((D_MODEL, tn), lambda j: (0, j)),
                  pl.BlockSpec((1, tn), lambda j: (0, j))],
        out_specs=pl.BlockSpec((rows, tn), lambda j: (0, j)),
        compiler_params=pltpu.CompilerParams(dimension_semantics=("arbitrary",)),
        name="adaln",
    )(c_pad, w_ada, b_ada)


def _ffn_kernel(x_ref, mod_ref, gpre_ref, gpost_ref, wgu_ref, wdn_ref, o_ref, *, mod0):
    x = x_ref[...]
    sh = mod_ref[0, mod0:mod0 + 1, :]
    sc = mod_ref[0, mod0 + 1:mod0 + 2, :]
    gt = mod_ref[0, mod0 + 2:mod0 + 3, :]
    h = (_rms(x) * gpre_ref[...]) * (1.0 + sc) + sh
    hb = h.astype(BF16)
    acc = None
    for c0, cw in FFN_CHUNKS:
        g = jnp.dot(hb, wgu_ref[:, c0:c0 + cw], preferred_element_type=F32)
        u = jnp.dot(hb, wgu_ref[:, D_FF + c0:D_FF + c0 + cw], preferred_element_type=F32)
        a = (g * jax.nn.sigmoid(g) * u).astype(BF16)
        part = jnp.dot(a, wdn_ref[c0:c0 + cw, :], preferred_element_type=F32)
        acc = part if acc is None else acc + part
    o_ref[...] = x + (0.5 * gt) * (_rms(acc) * gpost_ref[...])


def _ffn(x2d, mods, g_pre, g_post, w_gu, w_dn, *, mod0, seq):
    n = x2d.shape[0]
    tiles_per_batch = seq // FFN_TM
    return pl.pallas_call(
        functools.partial(_ffn_kernel, mod0=mod0),
        out_shape=jax.ShapeDtypeStruct((n, D_MODEL), F32),
        grid=(n // FFN_TM,),
        in_specs=[pl.BlockSpec((FFN_TM, D_MODEL), lambda i: (i, 0)),
                  pl.BlockSpec((1, N_MOD, D_MODEL), lambda i: (i // tiles_per_batch, 0, 0)),
                  _resident((1, D_MODEL)),
                  _resident((1, D_MODEL)),
                  _resident((D_MODEL, 2 * D_FF)),
                  _resident((D_FF, D_MODEL))],
        out_specs=pl.BlockSpec((FFN_TM, D_MODEL), lambda i: (i, 0)),
        compiler_params=pltpu.CompilerParams(dimension_semantics=("arbitrary",),
                                             vmem_limit_bytes=VMEM_LIMIT),
        name="ffn",
    )(x2d, mods, g_pre, g_post, w_gu, w_dn)


def _proj_kernel(x_ref, mod_ref, gpre_ref, w_ref, wa2_ref, ba_ref, bsm_ref, eqk_ref,
                 qa_ref, ka_ref, va_ref, la_ref, ra_ref,
                 qe_ref, qo_ref, ke_ref, ko_ref, vb_ref, sga_ref, sgb_ref,
                 fcar_ref, *, tiles_per_batch):
    i = pl.program_id(0)
    tm = x_ref.shape[0]

    @pl.when(i % tiles_per_batch == 0)
    def _():
        fcar_ref[...] = jnp.zeros_like(fcar_ref)

    x = x_ref[...]
    sh = mod_ref[0, 3:4, :]
    sc = mod_ref[0, 4:5, :]
    hb = ((_rms(x) * gpre_ref[...]) * (1.0 + sc) + sh).astype(BF16)

    def proj(c0, cw):
        return jnp.dot(hb, w_ref[:, c0:c0 + cw], preferred_element_type=F32)

    qa_ref[...] = proj(_C_QA, GLA_DK) * (GLA_HK ** -0.5)
    ka_ref[...] = proj(_C_KA, GLA_DK)
    va_ref[...] = proj(_C_VA, GLA_DV).astype(BF16)
    r = proj(_C_RA, GLA_DV)
    ra_ref[...] = (r * jax.nn.sigmoid(r)).astype(BF16)

    zs = proj(_C_SMALL, LANES)
    xa = jnp.dot(zs.astype(BF16), wa2_ref[...], preferred_element_type=F32) + ba_ref[...]
    la_ref[...] = _log_sigmoid(xa) * (1.0 / GLA_TAU)

    lane = lax.broadcasted_iota(jnp.int32, (tm, LANES), 1)
    row = lax.broadcasted_iota(jnp.int32, (tm, LANES), 0)
    in_f = (lane >= _F_LANE0) & (lane < _F_LANE0 + FOX_HEADS)
    f = jnp.where(in_f, _log_sigmoid(zs + bsm_ref[...]), 0.0)
    shift = 1
    while shift < tm:
        f = f + jnp.where(row >= shift, pltpu.roll(f, shift, 0), 0.0)
        shift *= 2
    f = f + fcar_ref[...]
    fcar_ref[...] = f[tm - 1:tm, :]

    f2 = f * LOG2E
    p0 = f2.astype(BF16).astype(F32)
    r1 = f2 - p0
    p1 = r1.astype(BF16).astype(F32)
    p2 = (r1 - p1).astype(BF16).astype(F32)
    fc = p0 + pltpu.roll(p1, FOX_HEADS, 1) + pltpu.roll(p2, 2 * FOX_HEADS, 1)
    fc = jnp.where(lane == 0, 1.0, fc)
    aug = jnp.dot(fc.astype(BF16), eqk_ref[...], preferred_element_type=F32)
    augq = aug[:, :FOX_W]
    augk = aug[:, FOX_W:]

    lane_w = lax.broadcasted_iota(jnp.int32, (tm, FOX_W), 1)
    low_half = (lane_w & (LANES - 1)) < FOX_HD
    zq = proj(_C_QB, FOX_W) * (FOX_HD ** -0.5 * LOG2E)
    qe_ref[...] = jnp.where(low_half, zq, augq).astype(BF16)
    qo_ref[...] = jnp.where(low_half, augq, zq).astype(BF16)
    zk = proj(_C_KB, FOX_W)
    ke_ref[...] = jnp.where(low_half, zk, augk).astype(BF16)
    ko_ref[...] = jnp.where(low_half, augk, zk).astype(BF16)
    vb_ref[...] = proj(_C_VB, FOX_W).astype(BF16)
    sga_ref[...] = jax.nn.sigmoid(proj(_C_GA, D_MODEL)).astype(BF16)
    sgb_ref[...] = jax.nn.sigmoid(proj(_C_GB, D_MODEL)).astype(BF16)


def _proj(x2d, mods, g_pre, w_all, w_a2p, b_a, b_small, e_qk, *, seq):
    n = x2d.shape[0]
    tm = PROJ_TM
    tiles_per_batch = seq // tm
    row_spec = lambda w: pl.BlockSpec((tm, w), lambda i: (i, 0))
    out_shapes = [
        jax.ShapeDtypeStruct((n, GLA_DK), F32),
        jax.ShapeDtypeStruct((n, GLA_DK), F32),
        jax.ShapeDtypeStruct((n, GLA_DV), BF16),
        jax.ShapeDtypeStruct((n, GLA_DK), F32),
        jax.ShapeDtypeStruct((n, GLA_DV), BF16),
        jax.ShapeDtypeStruct((n, FOX_W), BF16),
        jax.ShapeDtypeStruct((n, FOX_W), BF16),
        jax.ShapeDtypeStruct((n, FOX_W), BF16),
        jax.ShapeDtypeStruct((n, FOX_W), BF16),
        jax.ShapeDtypeStruct((n, FOX_W), BF16),
        jax.ShapeDtypeStruct((n, D_MODEL), BF16),
        jax.ShapeDtypeStruct((n, D_MODEL), BF16),
    ]
    return pl.pallas_call(
        functools.partial(_proj_kernel, tiles_per_batch=tiles_per_batch),
        out_shape=out_shapes,
        grid=(n // tm,),
        in_specs=[row_spec(D_MODEL),
                  pl.BlockSpec((1, N_MOD, D_MODEL), lambda i: (i // tiles_per_batch, 0, 0)),
                  _resident((1, D_MODEL)),
                  _resident((D_MODEL, _W_COLS)),
                  _resident((LANES, GLA_DK)),
                  _resident((1, GLA_DK)),
                  _resident((1, LANES)),
                  _resident((LANES, 2 * FOX_W))],
        out_specs=[row_spec(s.shape[1]) for s in out_shapes],
        scratch_shapes=[pltpu.VMEM((1, LANES), F32)],
        compiler_params=pltpu.CompilerParams(dimension_semantics=("arbitrary",),
                                             vmem_limit_bytes=VMEM_LIMIT),
        name="proj",
    )(x2d, mods, g_pre, w_all, w_a2p, b_a, b_small, e_qk)


def _gla_tables():
    c = GLA_CHUNK
    t = np.arange(c)[:, None]
    u = np.arange(c)[None, :]
    blocks = [(u <= t), (u > t)]
    masks = [(t == u)]
    for s in GLA_LEVELS:
        piv = (t // (2 * s)) * (2 * s) + s
        upper = t >= piv
        blocks.append(np.where(upper, (u > piv) & (u <= t), (u > t) & (u <= piv)))
        same = (t // (2 * s)) == (u // (2 * s))
        masks.append(same & (t % (2 * s) >= s) & (u % (2 * s) < s))
    rmat = np.concatenate(blocks, axis=0).astype(np.float32)
    mask = np.stack(masks, axis=0).astype(np.float32)
    return rmat, mask


def _dot_nt(a, b):
    return lax.dot_general(a, b, (((1,), (1,)), ((), ())), preferred_element_type=F32)


def _gla_kernel(q_ref, k_ref, v_ref, g_ref, r_ref, gg_ref, rmat_ref, mask_ref, o_ref, s_ref):
    c = GLA_CHUNK

    @pl.when(pl.program_id(2) == 0)
    def _():
        s_ref[...] = jnp.zeros_like(s_ref)

    state = s_ref[...]
    for ci in range(q_ref.shape[0] // c):
        sl = pl.ds(ci * c, c)
        q = q_ref[sl, :]
        k = k_ref[sl, :]
        g = g_ref[sl, :]
        v = v_ref[sl, :]
        g_hi = g.astype(BF16)
        g_lo = (g - g_hi.astype(F32)).astype(BF16)
        x2 = jnp.dot(rmat_ref[...], jnp.concatenate([g_hi, g_lo], axis=1),
                     preferred_element_type=F32)
        xs = x2[:, :GLA_HK] + x2[:, GLA_HK:]
        eb = jnp.exp(xs[0:c])
        ebl = jnp.exp(xs[c:2 * c])

        a = mask_ref[0] * _dot_nt(q.astype(BF16), k.astype(BF16))
        for li in range(len(GLA_LEVELS)):
            f = jnp.exp(xs[(2 + li) * c:(3 + li) * c])
            a = a + mask_ref[li + 1] * _dot_nt((q * f).astype(BF16), (k * f).astype(BF16))

        lhs = jnp.concatenate([(q * eb).astype(BF16), a.astype(BF16)], axis=1)
        rhs = jnp.concatenate([state.astype(BF16), v], axis=0)
        o = jnp.dot(lhs, rhs, preferred_element_type=F32)

        upd = jnp.dot((k * ebl).T.astype(BF16), v, preferred_element_type=F32)
        decay = eb.T[:, c - 1:c]
        state = decay * state + upd

        on = _rms(o) * gg_ref[0]
        o_ref[sl, :] = (on * r_ref[sl, :].astype(F32)).astype(BF16)
    s_ref[...] = state


def _gla(qa, ka, va, la, ra, g_gla, rmat, mask, *, batch, seq):
    n = qa.shape[0]
    tiles = seq // GLA_TT
    rows = lambda b, h, t: b * tiles + t
    return pl.pallas_call(
        _gla_kernel,
        out_shape=jax.ShapeDtypeStruct((n, GLA_DV), BF16),
        grid=(batch, GLA_HEADS, tiles),
        in_specs=[pl.BlockSpec((GLA_TT, GLA_HK), lambda b, h, t: (rows(b, h, t), h)),
                  pl.BlockSpec((GLA_TT, GLA_HK), lambda b, h, t: (rows(b, h, t), h)),
                  pl.BlockSpec((GLA_TT, GLA_HV), lambda b, h, t: (rows(b, h, t), h)),
                  pl.BlockSpec((GLA_TT, GLA_HK), lambda b, h, t: (rows(b, h, t), h)),
                  pl.BlockSpec((GLA_TT, GLA_HV), lambda b, h, t: (rows(b, h, t), h)),
                  pl.BlockSpec((1, 1, GLA_HV), lambda b, h, t: (h, 0, 0)),
                  _resident(rmat.shape),
                  _resident(mask.shape)],
        out_specs=pl.BlockSpec((GLA_TT, GLA_HV), lambda b, h, t: (rows(b, h, t), h)),
        scratch_shapes=[pltpu.VMEM((GLA_HK, GLA_HV), F32)],
        compiler_params=pltpu.CompilerParams(
            dimension_semantics=("arbitrary", "arbitrary", "arbitrary"),
            vmem_limit_bytes=VMEM_LIMIT),
        name="gla",
    )(qa, ka, va, la, ra, g_gla, rmat, mask)


def _fox_kernel(qe_ref, qo_ref, ke_ref, ko_ref, v_ref, o_ref, m_ref, acc_ref, sa_ref, sb_ref):
    i = pl.program_id(2)
    bq = qe_ref.shape[0]
    bk = bq
    q_refs = (qe_ref, qo_ref)
    k_refs = (ke_ref, ko_ref)

    m_ref[...] = jnp.full_like(m_ref, -jnp.inf)
    acc_ref[...] = jnp.zeros_like(acc_ref)

    def tile(t):
        return pl.ds(pl.multiple_of(t * bk, bk), bk)

    def logits(dst_ref, t):
        for hd in range(2):
            dst_ref[hd] = _dot_nt(q_refs[hd][...], k_refs[hd][tile(t), :])

    def consume(src_ref, t, masked):
        vp = v_ref[tile(t), :]
        vext = jnp.concatenate([vp, jnp.ones_like(vp)], axis=1)
        for hd in range(2):
            s = src_ref[hd]
            if masked:
                r_i = lax.broadcasted_iota(jnp.int32, (bq, bk), 0)
                c_i = lax.broadcasted_iota(jnp.int32, (bq, bk), 1)
                s = jnp.where(c_i <= r_i, s, -jnp.inf)
            m_old = m_ref[hd]
            m_new = jnp.maximum(m_old, jnp.max(s, axis=1, keepdims=True))
            p = jnp.exp2(s - m_new).astype(BF16)
            acc_ref[hd] = jnp.exp2(m_old - m_new) * acc_ref[hd] + jnp.dot(
                p, vext, preferred_element_type=F32)
            m_ref[hd] = m_new

    logits(sa_ref, 0)

    def pair(u, carry):
        logits(sb_ref, 2 * u + 1)
        consume(sa_ref, 2 * u, False)
        logits(sa_ref, 2 * u + 2)
        consume(sb_ref, 2 * u + 1, False)
        return carry

    lax.fori_loop(0, i // 2, pair, 0)

    @pl.when(i % 2 == 1)
    def _():
        logits(sb_ref, i)
        consume(sa_ref, i - 1, False)
        consume(sb_ref, i, True)

    @pl.when(i % 2 == 0)
    def _():
        consume(sa_ref, i, True)

    lane = lax.broadcasted_iota(jnp.int32, (bq, LANES), 1)
    oe = acc_ref[0]
    oo = acc_ref[1]
    o_ref[...] = jnp.where(lane < FOX_HD,
                           oe[:, :LANES] / oe[:, LANES:],
                           oo[:, :LANES] / oo[:, LANES:]).astype(BF16)


def _fox(qe, qo, ke, ko, vb, *, batch, seq):
    n = qe.shape[0]
    bq = FOX_BQ
    tiles = seq // bq
    q_spec = pl.BlockSpec((bq, LANES), lambda b, j, i: (b * tiles + i, j))
    kv_spec = pl.BlockSpec((seq, LANES), lambda b, j, i: (b, j))
    return pl.pallas_call(
        _fox_kernel,
        out_shape=jax.ShapeDtypeStruct((n, FOX_W), BF16),
        grid=(batch, FOX_PAIRS, tiles),
        in_specs=[q_spec, q_spec, kv_spec, kv_spec, kv_spec],
        out_specs=pl.BlockSpec((bq, LANES), lambda b, j, i: (b * tiles + i, j)),
        scratch_shapes=[pltpu.VMEM((2, bq, 1), F32),
                        pltpu.VMEM((2, bq, 2 * LANES), F32),
                        pltpu.VMEM((2, bq, bq), F32),
                        pltpu.VMEM((2, bq, bq), F32)],
        compiler_params=pltpu.CompilerParams(
            dimension_semantics=("arbitrary", "arbitrary", "arbitrary"),
            vmem_limit_bytes=VMEM_LIMIT),
        name="fox",
    )(qe, qo, ke, ko, vb)


def _merge_kernel(x_ref, mod_ref, gpost_ref, oa_ref, ob_ref, sga_ref, sgb_ref,
                  wpa_ref, wpb_ref, wout_ref, o_ref):
    pa = jnp.dot(oa_ref[...], wpa_ref[...], preferred_element_type=F32)
    pb = jnp.dot(ob_ref[...], wpb_ref[...], preferred_element_type=F32)
    merged = sga_ref[...].astype(F32) * pa + sgb_ref[...].astype(F32) * pb
    y = jnp.dot(merged.astype(BF16), wout_ref[...], preferred_element_type=F32)
    gt = mod_ref[0, 5:6, :]
    o_ref[...] = x_ref[...] + gt * (_rms(y) * gpost_ref[...])


def _merge(x2d, mods, g_post, oa, ob, sga, sgb, w_pa, w_pb, w_out, *, seq):
    n = x2d.shape[0]
    tm = MERGE_TM
    tiles_per_batch = seq // tm
    row_spec = pl.BlockSpec((tm, D_MODEL), lambda i: (i, 0))
    return pl.pallas_call(
        _merge_kernel,
        out_shape=jax.ShapeDtypeStruct((n, D_MODEL), F32),
        grid=(n // tm,),
        in_specs=[row_spec,
                  pl.BlockSpec((1, N_MOD, D_MODEL), lambda i: (i // tiles_per_batch, 0, 0)),
                  _resident((1, D_MODEL)),
                  row_spec, row_spec, row_spec, row_spec,
                  _resident((D_MODEL, D_MODEL)),
                  _resident((D_MODEL, D_MODEL)),
                  _resident((D_MODEL, D_MODEL))],
        out_specs=row_spec,
        compiler_params=pltpu.CompilerParams(dimension_semantics=("arbitrary",),
                                             vmem_limit_bytes=VMEM_LIMIT),
        name="merge",
    )(x2d, mods, g_post, oa, ob, sga, sgb, w_pa, w_pb, w_out)


def _fox_bias_placement():
    e = np.zeros((LANES, 2 * FOX_W), np.float32)
    for h in range(FOX_HEADS):
        base = LANES * (h // 2) + (FOX_HD if h % 2 == 0 else 0)
        for p in range(3):
            src = _F_LANE0 + FOX_HEADS * p + h
            e[src, base + p] = 1.0
            e[0, base + 3 + p] = 1.0
            e[0, FOX_W + base + p] = 1.0
            e[src, FOX_W + base + 3 + p] = -1.0
    return e


def kernel(x, c, w_ada, b_ada, g_pre, g_post, w_gu1, w_dn1, w_gu2, w_dn2,
           w_in, w_a2, b_a, b_f, g_gla, w_pa, w_pb, w_out):
    batch, seq, d = x.shape
    n = batch * seq
    depth = w_ada.shape[0]
    rmat_np, mask_np = _gla_tables()
    rmat = jnp.asarray(rmat_np, BF16)
    mask = jnp.asarray(mask_np, F32)
    e_qk = jnp.asarray(_fox_bias_placement(), BF16)

    x2d = x.reshape(n, d)
    c_pad = jnp.pad(c, ((0, 16 - batch), (0, 0)))
    for l in range(depth):
        mods = _adaln(c_pad, w_ada[l], b_ada[l][None, :])[:batch].reshape(batch, N_MOD, d)

        x2d = _ffn(x2d, mods, g_pre[l, 0][None, :], g_post[l, 0][None, :],
                   w_gu1[l].astype(BF16), w_dn1[l].astype(BF16), mod0=0, seq=seq)

        wi = w_in[l]
        cuts = np.cumsum((GLA_DK, GLA_DK, GLA_DV, GLA_RANK, GLA_DV, FOX_W, FOX_W, FOX_W,
                          FOX_HEADS, D_MODEL, D_MODEL))[:-1].tolist()
        (w_qa, w_ka, w_va, w_al, w_ra, w_qb, w_kb, w_vb, w_fb, w_ga, w_gb) = jnp.split(wi, cuts, axis=1)
        w_small = jnp.concatenate(
            [w_al, w_fb, jnp.zeros((d, LANES - GLA_RANK - FOX_HEADS), wi.dtype)], axis=1)
        w_all = jnp.concatenate(
            [w_qa, w_ka, w_va, w_ra, w_qb, w_kb, w_vb, w_ga, w_gb, w_small], axis=1).astype(BF16)
        w_a2p = jnp.pad(w_a2[l], ((0, LANES - GLA_RANK), (0, 0))).astype(BF16)
        b_small = jnp.pad(b_f[l], (_F_LANE0, LANES - _F_LANE0 - FOX_HEADS))[None, :]

        (qa, ka, va, la, ra, qe, qo, ke, ko, vb, sga, sgb) = _proj(
            x2d, mods, g_pre[l, 1][None, :], w_all, w_a2p, b_a[l][None, :], b_small, e_qk, seq=seq)

        oa = _gla(qa, ka, va, la, ra, g_gla[l].reshape(GLA_HEADS, 1, GLA_HV), rmat, mask,
                  batch=batch, seq=seq)
        ob = _fox(qe, qo, ke, ko, vb, batch=batch, seq=seq)

        x2d = _merge(x2d, mods, g_post[l, 1][None, :], oa, ob, sga, sgb,
                     w_pa[l].astype(BF16), w_pb[l].astype(BF16), w_out[l].astype(BF16), seq=seq)

        x2d = _ffn(x2d, mods, g_pre[l, 2][None, :], g_post[l, 2][None, :],
                   w_gu2[l].astype(BF16), w_dn2[l].astype(BF16), mod0=6, seq=seq)
    return x2d.reshape(batch, seq, d)
```

```python
import functools

import numpy as np
import jax
import jax.numpy as jnp
from jax import lax
from jax.experimental import pallas as pl
from jax.experimental.pallas import tpu as pltpu

F32 = jnp.float32
BF16 = jnp.bfloat16

EPS = 1e-6
LOG2E = 1.4426950408889634

D_MODEL = 1024
D_FF = 2816
N_MOD = 9
GLA_HEADS = 4
GLA_DK = 512
GLA_DV = 1024
GLA_HK = GLA_DK // GLA_HEADS
GLA_HV = GLA_DV // GLA_HEADS
GLA_RANK = 16
GLA_TAU = 16.0
FOX_HEADS = 16
FOX_HD = 64
FOX_W = FOX_HEADS * FOX_HD
FOX_PAIRS = FOX_HEADS // 2

LANES = 128
VMEM_LIMIT = 56 * 1024 * 1024

FFN_TM = 512
FFN_CHUNKS = ((0, 1024), (1024, 1024), (2048, 768))
PROJ_TM = 256
GLA_CHUNK = 128
GLA_TT = 1024
GLA_LEVELS = (64, 32, 16, 8, 4, 2, 1)
FOX_BQ = 512
_FOX_SUM_ROWS = 16
MERGE_TM = 512

_C_QA, _C_KA, _C_VA, _C_RA = 0, 512, 1024, 2048
_C_QB, _C_KB, _C_VB, _C_GA, _C_GB = 3072, 4096, 5120, 6144, 7168
_C_SMALL = 8192
_W_COLS = 8320
_F_LANE0 = 16


def _resident(shape):
    nd = len(shape)
    return pl.BlockSpec(shape, lambda *_: (0,) * nd, pipeline_mode=pl.Buffered(1))


def _rms(x):
    return x * lax.rsqrt(jnp.mean(x * x, axis=-1, keepdims=True) + EPS)


def _log_sigmoid(x):
    return jnp.minimum(x, 0.0) - jnp.log(1.0 + jnp.exp(-jnp.abs(x)))


def _adaln_kernel(c_ref, w_ref, b_ref, o_ref):
    c = c_ref[...]
    s = c * jax.nn.sigmoid(c)
    o_ref[...] = jnp.dot(s.astype(BF16), w_ref[...].astype(BF16),
                         preferred_element_type=F32) + b_ref[...]


def _adaln(c_pad, w_ada, b_ada):
    rows = c_pad.shape[0]
    ncol = w_ada.shape[1]
    tn = 1024
    return pl.pallas_call(
        _adaln_kernel,
        out_shape=jax.ShapeDtypeStruct((rows, ncol), F32),
        grid=(ncol // tn,),
        in_specs=[pl.BlockSpec((rows, D_MODEL), lambda j: (0, 0)),
                  pl.BlockSpec((D_MODEL, tn), lambda j: (0, j)),
                  pl.BlockSpec((1, tn), lambda j: (0, j))],
        out_specs=pl.BlockSpec((rows, tn), lambda j: (0, j)),
        compiler_params=pltpu.CompilerParams(dimension_semantics=("arbitrary",)),
        name="adaln",
    )(c_pad, w_ada, b_ada)


def _ffn_kernel(x_ref, mod_ref, gpre_ref, gpost_ref, wgu_ref, wdn_ref, o_ref, *, mod0):
    x = x_ref[...]
    sh = mod_ref[0, mod0:mod0 + 1, :]
    sc = mod_ref[0, mod0 + 1:mod0 + 2, :]
    gt = mod_ref[0, mod0 + 2:mod0 + 3, :]
    h = (_rms(x) * gpre_ref[...]) * (1.0 + sc) + sh
    hb = h.astype(BF16)
    acc = None
    for c0, cw in FFN_CHUNKS:
        g = jnp.dot(hb, wgu_ref[:, c0:c0 + cw], preferred_element_type=F32)
        u = jnp.dot(hb, wgu_ref[:, D_FF + c0:D_FF + c0 + cw], preferred_element_type=F32)
        a = (g * jax.nn.sigmoid(g) * u).astype(BF16)
        part = jnp.dot(a, wdn_ref[c0:c0 + cw, :], preferred_element_type=F32)
        acc = part if acc is None else acc + part
    o_ref[...] = x + (0.5 * gt) * (_rms(acc) * gpost_ref[...])


def _ffn(x2d, mods, g_pre, g_post, w_gu, w_dn, *, mod0, seq):
    n = x2d.shape[0]
    tiles_per_batch = seq // FFN_TM
    return pl.pallas_call(
        functools.partial(_ffn_kernel, mod0=mod0),
        out_shape=jax.ShapeDtypeStruct((n, D_MODEL), F32),
        grid=(n // FFN_TM,),
        in_specs=[pl.BlockSpec((FFN_TM, D_MODEL), lambda i: (i, 0)),
                  pl.BlockSpec((1, N_MOD, D_MODEL), lambda i: (i // tiles_per_batch, 0, 0)),
                  _resident((1, D_MODEL)),
                  _resident((1, D_MODEL)),
                  _resident((D_MODEL, 2 * D_FF)),
                  _resident((D_FF, D_MODEL))],
        out_specs=pl.BlockSpec((FFN_TM, D_MODEL), lambda i: (i, 0)),
        compiler_params=pltpu.CompilerParams(dimension_semantics=("arbitrary",),
                                             vmem_limit_bytes=VMEM_LIMIT),
        name="ffn",
    )(x2d, mods, g_pre, g_post, w_gu, w_dn)


def _proj_kernel(x_ref, mod_ref, gpre_ref, w_ref, wa2_ref, ba_ref, bsm_ref, eqk_ref,
                 qa_ref, ka_ref, va_ref, la_ref, ra_ref,
                 qe_ref, qo_ref, ke_ref, ko_ref, vt_ref, sga_ref, sgb_ref,
                 fcar_ref, *, tiles_per_batch):
    i = pl.program_id(0)
    tm = x_ref.shape[0]

    @pl.when(i % tiles_per_batch == 0)
    def _():
        fcar_ref[...] = jnp.zeros_like(fcar_ref)

    x = x_ref[...]
    sh = mod_ref[0, 3:4, :]
    sc = mod_ref[0, 4:5, :]
    hb = ((_rms(x) * gpre_ref[...]) * (1.0 + sc) + sh).astype(BF16)

    def proj(c0, cw):
        return jnp.dot(hb, w_ref[:, c0:c0 + cw], preferred_element_type=F32)

    qa_ref[...] = proj(_C_QA, GLA_DK) * (GLA_HK ** -0.5)
    ka_ref[...] = proj(_C_KA, GLA_DK)
    va_ref[...] = proj(_C_VA, GLA_DV).astype(BF16)
    r = proj(_C_RA, GLA_DV)
    ra_ref[...] = (r * jax.nn.sigmoid(r)).astype(BF16)

    zs = proj(_C_SMALL, LANES)
    xa = jnp.dot(zs.astype(BF16), wa2_ref[...], preferred_element_type=F32) + ba_ref[...]
    la_ref[...] = _log_sigmoid(xa) * (1.0 / GLA_TAU)

    lane = lax.broadcasted_iota(jnp.int32, (tm, LANES), 1)
    row = lax.broadcasted_iota(jnp.int32, (tm, LANES), 0)
    in_f = (lane >= _F_LANE0) & (lane < _F_LANE0 + FOX_HEADS)
    f = jnp.where(in_f, _log_sigmoid(zs + bsm_ref[...]), 0.0)
    shift = 1
    while shift < tm:
        f = f + jnp.where(row >= shift, pltpu.roll(f, shift, 0), 0.0)
        shift *= 2
    f = f + fcar_ref[...]
    fcar_ref[...] = f[tm - 1:tm, :]

    f2 = f * LOG2E
    p0 = f2.astype(BF16).astype(F32)
    r1 = f2 - p0
    p1 = r1.astype(BF16).astype(F32)
    p2 = (r1 - p1).astype(BF16).astype(F32)
    fc = p0 + pltpu.roll(p1, FOX_HEADS, 1) + pltpu.roll(p2, 2 * FOX_HEADS, 1)
    fc = jnp.where(lane == 0, 1.0, fc)
    aug = jnp.dot(fc.astype(BF16), eqk_ref[...], preferred_element_type=F32)
    augq = aug[:, :FOX_W]
    augk = aug[:, FOX_W:]

    lane_w = lax.broadcasted_iota(jnp.int32, (tm, FOX_W), 1)
    low_half = (lane_w & (LANES - 1)) < FOX_HD
    zq = proj(_C_QB, FOX_W) * (FOX_HD ** -0.5 * LOG2E)
    qe_ref[...] = jnp.where(low_half, zq, augq).astype(BF16)
    qo_ref[...] = jnp.where(low_half, augq, zq).astype(BF16)
    zk = proj(_C_KB, FOX_W)
    ke_ref[...] = jnp.where(low_half, zk, augk).astype(BF16)
    ko_ref[...] = jnp.where(low_half, augk, zk).astype(BF16)
    zv = proj(_C_VB, FOX_W)
    for j in range(FOX_PAIRS):
        vt_ref[0, j, 0] = zv[:, j * LANES:(j + 1) * LANES].T.astype(BF16)
    sga_ref[...] = jax.nn.sigmoid(proj(_C_GA, D_MODEL)).astype(BF16)
    sgb_ref[...] = jax.nn.sigmoid(proj(_C_GB, D_MODEL)).astype(BF16)


def _proj(x2d, mods, g_pre, w_all, w_a2p, b_a, b_small, e_qk, *, batch, seq):
    n = x2d.shape[0]
    tm = PROJ_TM
    tiles_per_batch = seq // tm
    row_spec = lambda w: pl.BlockSpec((tm, w), lambda i: (i, 0))
    out_shapes = [
        jax.ShapeDtypeStruct((n, GLA_DK), F32),
        jax.ShapeDtypeStruct((n, GLA_DK), F32),
        jax.ShapeDtypeStruct((n, GLA_DV), BF16),
        jax.ShapeDtypeStruct((n, GLA_DK), F32),
        jax.ShapeDtypeStruct((n, GLA_DV), BF16),
        jax.ShapeDtypeStruct((n, FOX_W), BF16),
        jax.ShapeDtypeStruct((n, FOX_W), BF16),
        jax.ShapeDtypeStruct((n, FOX_W), BF16),
        jax.ShapeDtypeStruct((n, FOX_W), BF16),
        jax.ShapeDtypeStruct((batch, FOX_PAIRS, tiles_per_batch, LANES, tm), BF16),
        jax.ShapeDtypeStruct((n, D_MODEL), BF16),
        jax.ShapeDtypeStruct((n, D_MODEL), BF16),
    ]
    return pl.pallas_call(
        functools.partial(_proj_kernel, tiles_per_batch=tiles_per_batch),
        out_shape=out_shapes,
        grid=(n // tm,),
        in_specs=[row_spec(D_MODEL),
                  pl.BlockSpec((1, N_MOD, D_MODEL), lambda i: (i // tiles_per_batch, 0, 0)),
                  _resident((1, D_MODEL)),
                  _resident((D_MODEL, _W_COLS)),
                  _resident((LANES, GLA_DK)),
                  _resident((1, GLA_DK)),
                  _resident((1, LANES)),
                  _resident((LANES, 2 * FOX_W))],
        out_specs=[row_spec(s.shape[1]) if len(s.shape) == 2 else
                   pl.BlockSpec((1, FOX_PAIRS, 1, LANES, tm),
                                lambda i: (i // tiles_per_batch, 0, i % tiles_per_batch, 0, 0))
                   for s in out_shapes],
        scratch_shapes=[pltpu.VMEM((1, LANES), F32)],
        compiler_params=pltpu.CompilerParams(dimension_semantics=("arbitrary",),
                                             vmem_limit_bytes=VMEM_LIMIT),
        name="proj",
    )(x2d, mods, g_pre, w_all, w_a2p, b_a, b_small, e_qk)


def _gla_tables():
    c = GLA_CHUNK
    t = np.arange(c)[:, None]
    u = np.arange(c)[None, :]
    masks = [(t == u)]
    for s in GLA_LEVELS:
        same = (t // (2 * s)) == (u // (2 * s))
        masks.append(same & (t % (2 * s) >= s) & (u % (2 * s) < s))
    ltri = (u <= t).astype(np.float32)
    mask = np.stack(masks, axis=0).astype(np.float32)
    return ltri, mask


def _dot_nt(a, b):
    return lax.dot_general(a, b, (((1,), (1,)), ((), ())), preferred_element_type=F32)


def _gla_pivots(b, b_rows_ref, s):
    c, dk = b.shape
    if s >= 8:
        return jnp.concatenate(
            [jnp.broadcast_to(b_rows_ref[pl.ds(g0 + s, 1), :], (2 * s, dk))
             for g0 in range(0, c, 2 * s)], axis=0)
    b3 = b.reshape(c // 8, 8, dk)
    sub = lax.broadcasted_iota(jnp.int32, b3.shape, 1)
    offset = s - (sub & (2 * s - 1))
    piv = b3
    for d in range(-(s - 1), s + 1):
        if d != 0:
            piv = jnp.where(offset == d, pltpu.roll(b3, (-d) % 8, 1), piv)
    return piv.reshape(c, dk)


def _gla_kernel(q_ref, k_ref, v_ref, g_ref, r_ref, gg_ref, ltri_ref, mask_ref, o_ref, s_ref, b_ref):
    c = GLA_CHUNK

    @pl.when(pl.program_id(2) == 0)
    def _():
        s_ref[...] = jnp.zeros_like(s_ref)

    state = s_ref[...]
    for ci in range(q_ref.shape[0] // c):
        sl = pl.ds(ci * c, c)
        q = q_ref[sl, :]
        k = k_ref[sl, :]
        g = g_ref[sl, :]
        v = v_ref[sl, :]
        g_hi = g.astype(BF16)
        g_lo = (g - g_hi.astype(F32)).astype(BF16)
        x2 = jnp.dot(ltri_ref[...], jnp.concatenate([g_hi, g_lo], axis=1),
                     preferred_element_type=F32)
        b = (x2[:, :GLA_HK] + x2[:, GLA_HK:]) * LOG2E
        b_ref[ci] = b
        eb = jnp.exp2(b)
        ebl = jnp.exp2(b_ref[ci, pl.ds(c - 1, 1), :] - b)

        a = mask_ref[0] * _dot_nt(q.astype(BF16), k.astype(BF16))
        for li, s in enumerate(GLA_LEVELS):
            f = jnp.exp2(-jnp.abs(b - _gla_pivots(b, b_ref.at[ci], s)))
            a = a + mask_ref[li + 1] * _dot_nt((q * f).astype(BF16), (k * f).astype(BF16))

        lhs = jnp.concatenate([(q * eb).astype(BF16), a.astype(BF16)], axis=1)
        rhs = jnp.concatenate([state.astype(BF16), v], axis=0)
        o = jnp.dot(lhs, rhs, preferred_element_type=F32)

        upd = jnp.dot((k * ebl).T.astype(BF16), v, preferred_element_type=F32)
        decay = eb.T[:, c - 1:c]
        state = decay * state + upd

        on = _rms(o) * gg_ref[0]
        o_ref[sl, :] = (on * r_ref[sl, :].astype(F32)).astype(BF16)
    s_ref[...] = state


def _gla(qa, ka, va, la, ra, g_gla, ltri, mask, *, batch, seq):
    n = qa.shape[0]
    tiles = seq // GLA_TT
    rows = lambda b, h, t: b * tiles + t
    return pl.pallas_call(
        _gla_kernel,
        out_shape=jax.ShapeDtypeStruct((n, GLA_DV), BF16),
        grid=(batch, GLA_HEADS, tiles),
        in_specs=[pl.BlockSpec((GLA_TT, GLA_HK), lambda b, h, t: (rows(b, h, t), h)),
                  pl.BlockSpec((GLA_TT, GLA_HK), lambda b, h, t: (rows(b, h, t), h)),
                  pl.BlockSpec((GLA_TT, GLA_HV), lambda b, h, t: (rows(b, h, t), h)),
                  pl.BlockSpec((GLA_TT, GLA_HK), lambda b, h, t: (rows(b, h, t), h)),
                  pl.BlockSpec((GLA_TT, GLA_HV), lambda b, h, t: (rows(b, h, t), h)),
                  pl.BlockSpec((1, 1, GLA_HV), lambda b, h, t: (h, 0, 0)),
                  _resident(ltri.shape),
                  _resident(mask.shape)],
        out_specs=pl.BlockSpec((GLA_TT, GLA_HV), lambda b, h, t: (rows(b, h, t), h)),
        scratch_shapes=[pltpu.VMEM((GLA_HK, GLA_HV), F32),
                        pltpu.VMEM((GLA_TT // GLA_CHUNK, GLA_CHUNK, GLA_HK), F32)],
        compiler_params=pltpu.CompilerParams(
            dimension_semantics=("arbitrary", "arbitrary", "arbitrary"),
            vmem_limit_bytes=VMEM_LIMIT),
        name="gla",
    )(qa, ka, va, la, ra, g_gla, ltri, mask)


def _fox_kernel(qe_ref, qo_ref, ke_ref, ko_ref, vt_ref, o_ref, m_ref, acc_ref, sa_ref, sb_ref):
    i = pl.program_id(2)
    bq = qe_ref.shape[0]
    bk = bq
    n_vt = bk // vt_ref.shape[-1]
    q_refs = (qe_ref, qo_ref)
    k_refs = (ke_ref, ko_ref)

    m_ref[...] = jnp.full_like(m_ref, -jnp.inf)
    acc_ref[...] = jnp.zeros_like(acc_ref)

    def logits(dst_ref, t):
        ks = pl.ds(pl.multiple_of(t * bk, bk), bk)
        for hd in range(2):
            dst_ref[hd] = _dot_nt(k_refs[hd][ks, :], q_refs[hd][...])

    def consume(src_ref, t, masked):
        vt = jnp.concatenate([vt_ref[0, 0, t * n_vt + c] for c in range(n_vt)], axis=1)
        ones = jnp.ones((_FOX_SUM_ROWS, bk), BF16)
        for hd in range(2):
            vaug = jnp.concatenate([vt[hd * FOX_HD:(hd + 1) * FOX_HD], ones], axis=0)
            s = src_ref[hd]
            if masked:
                k_i = lax.broadcasted_iota(jnp.int32, (bk, bq), 0)
                q_i = lax.broadcasted_iota(jnp.int32, (bk, bq), 1)
                s = jnp.where(k_i <= q_i, s, -jnp.inf)
            m_old = m_ref[hd]
            m_new = jnp.maximum(m_old, jnp.max(s, axis=0, keepdims=True))
            p = jnp.exp2(s - m_new).astype(BF16)
            acc_ref[hd] = jnp.exp2(m_old - m_new) * acc_ref[hd] + jnp.dot(
                vaug, p, preferred_element_type=F32)
            m_ref[hd] = m_new

    logits(sa_ref, 0)

    def pair(u, carry):
        logits(sb_ref, 2 * u + 1)
        consume(sa_ref, 2 * u, False)
        logits(sa_ref, 2 * u + 2)
        consume(sb_ref, 2 * u + 1, False)
        return carry

    lax.fori_loop(0, i // 2, pair, 0)

    @pl.when(i % 2 == 1)
    def _():
        logits(sb_ref, i)
        consume(sa_ref, i - 1, False)
        consume(sb_ref, i, True)

    @pl.when(i % 2 == 0)
    def _():
        consume(sa_ref, i, True)

    oe = acc_ref[0]
    oo = acc_ref[1]
    ot = jnp.concatenate([oe[:FOX_HD] / oe[FOX_HD:FOX_HD + 1],
                          oo[:FOX_HD] / oo[FOX_HD:FOX_HD + 1]], axis=0)
    o_ref[...] = ot.T.astype(BF16)


def _fox(qe, qo, ke, ko, vt, *, batch, seq):
    n = qe.shape[0]
    bq = FOX_BQ
    tiles = seq // bq
    q_spec = pl.BlockSpec((bq, LANES), lambda b, j, i: (b * tiles + i, j))
    kv_spec = pl.BlockSpec((seq, LANES), lambda b, j, i: (b, j))
    return pl.pallas_call(
        _fox_kernel,
        out_shape=jax.ShapeDtypeStruct((n, FOX_W), BF16),
        grid=(batch, FOX_PAIRS, tiles),
        in_specs=[q_spec, q_spec, kv_spec, kv_spec,
                  pl.BlockSpec((1, 1) + vt.shape[2:], lambda b, j, i: (b, j, 0, 0, 0))],
        out_specs=pl.BlockSpec((bq, LANES), lambda b, j, i: (b * tiles + i, j)),
        scratch_shapes=[pltpu.VMEM((2, 1, bq), F32),
                        pltpu.VMEM((2, FOX_HD + _FOX_SUM_ROWS, bq), F32),
                        pltpu.VMEM((2, bq, bq), F32),
                        pltpu.VMEM((2, bq, bq), F32)],
        compiler_params=pltpu.CompilerParams(
            dimension_semantics=("arbitrary", "arbitrary", "arbitrary"),
            vmem_limit_bytes=VMEM_LIMIT),
        name="fox",
    )(qe, qo, ke, ko, vt)


def _merge_kernel(x_ref, mod_ref, gpost_ref, oa_ref, ob_ref, sga_ref, sgb_ref,
                  wpa_ref, wpb_ref, wout_ref, o_ref):
    pa = jnp.dot(oa_ref[...], wpa_ref[...], preferred_element_type=F32)
    pb = jnp.dot(ob_ref[...], wpb_ref[...], preferred_element_type=F32)
    merged = sga_ref[...].astype(F32) * pa + sgb_ref[...].astype(F32) * pb
    y = jnp.dot(merged.astype(BF16), wout_ref[...], preferred_element_type=F32)
    gt = mod_ref[0, 5:6, :]
    o_ref[...] = x_ref[...] + gt * (_rms(y) * gpost_ref[...])


def _merge(x2d, mods, g_post, oa, ob, sga, sgb, w_pa, w_pb, w_out, *, seq):
    n = x2d.shape[0]
    tm = MERGE_TM
    tiles_per_batch = seq // tm
    row_spec = pl.BlockSpec((tm, D_MODEL), lambda i: (i, 0))
    return pl.pallas_call(
        _merge_kernel,
        out_shape=jax.ShapeDtypeStruct((n, D_MODEL), F32),
        grid=(n // tm,),
        in_specs=[row_spec,
                  pl.BlockSpec((1, N_MOD, D_MODEL), lambda i: (i // tiles_per_batch, 0, 0)),
                  _resident((1, D_MODEL)),
                  row_spec, row_spec, row_spec, row_spec,
                  _resident((D_MODEL, D_MODEL)),
                  _resident((D_MODEL, D_MODEL)),
                  _resident((D_MODEL, D_MODEL))],
        out_specs=row_spec,
        compiler_params=pltpu.CompilerParams(dimension_semantics=("arbitrary",),
                                             vmem_limit_bytes=VMEM_LIMIT),
        name="merge",
    )(x2d, mods, g_post, oa, ob, sga, sgb, w_pa, w_pb, w_out)


def _fox_bias_placement():
    e = np.zeros((LANES, 2 * FOX_W), np.float32)
    for h in range(FOX_HEADS):
        base = LANES * (h // 2) + (FOX_HD if h % 2 == 0 else 0)
        for p in range(3):
            src = _F_LANE0 + FOX_HEADS * p + h
            e[src, base + p] = 1.0
            e[0, base + 3 + p] = 1.0
            e[0, FOX_W + base + p] = 1.0
            e[src, FOX_W + base + 3 + p] = -1.0
    return e


def kernel(x, c, w_ada, b_ada, g_pre, g_post, w_gu1, w_dn1, w_gu2, w_dn2,
           w_in, w_a2, b_a, b_f, g_gla, w_pa, w_pb, w_out):
    batch, seq, d = x.shape
    n = batch * seq
    depth = w_ada.shape[0]
    ltri_np, mask_np = _gla_tables()
    ltri = jnp.asarray(ltri_np, BF16)
    mask = jnp.asarray(mask_np, F32)
    e_qk = jnp.asarray(_fox_bias_placement(), BF16)

    x2d = x.reshape(n, d)
    c_pad = jnp.pad(c, ((0, 16 - batch), (0, 0)))
    for l in range(depth):
        mods = _adaln(c_pad, w_ada[l], b_ada[l][None, :])[:batch].reshape(batch, N_MOD, d)

        x2d = _ffn(x2d, mods, g_pre[l, 0][None, :], g_post[l, 0][None, :],
                   w_gu1[l].astype(BF16), w_dn1[l].astype(BF16), mod0=0, seq=seq)

        wi = w_in[l]
        cuts = np.cumsum((GLA_DK, GLA_DK, GLA_DV, GLA_RANK, GLA_DV, FOX_W, FOX_W, FOX_W,
                          FOX_HEADS, D_MODEL, D_MODEL))[:-1].tolist()
        (w_qa, w_ka, w_va, w_al, w_ra, w_qb, w_kb, w_vb, w_fb, w_ga, w_gb) = jnp.split(wi, cuts, axis=1)
        w_small = jnp.concatenate(
            [w_al, w_fb, jnp.zeros((d, LANES - GLA_RANK - FOX_HEADS), wi.dtype)], axis=1)
        w_all = jnp.concatenate(
            [w_qa, w_ka, w_va, w_ra, w_qb, w_kb, w_vb, w_ga, w_gb, w_small], axis=1).astype(BF16)
        w_a2p = jnp.pad(w_a2[l], ((0, LANES - GLA_RANK), (0, 0))).astype(BF16)
        b_small = jnp.pad(b_f[l], (_F_LANE0, LANES - _F_LANE0 - FOX_HEADS))[None, :]

        (qa, ka, va, la, ra, qe, qo, ke, ko, vt, sga, sgb) = _proj(
            x2d, mods, g_pre[l, 1][None, :], w_all, w_a2p, b_a[l][None, :], b_small, e_qk,
            batch=batch, seq=seq)

        oa = _gla(qa, ka, va, la, ra, g_gla[l].reshape(GLA_HEADS, 1, GLA_HV), ltri, mask,
                  batch=batch, seq=seq)
        ob = _fox(qe, qo, ke, ko, vt, batch=batch, seq=seq)

        x2d = _merge(x2d, mods, g_post[l, 1][None, :], oa, ob, sga, sgb,
                     w_pa[l].astype(BF16), w_pb[l].astype(BF16), w_out[l].astype(BF16), seq=seq)

        x2d = _ffn(x2d, mods, g_pre[l, 2][None, :], g_post[l, 2][None, :],
                   w_gu2[l].astype(BF16), w_dn2[l].astype(BF16), mod0=6, seq=seq)
    return x2d.reshape(batch, seq, d)
```

```python
import functools

import numpy as np
import jax
import jax.numpy as jnp
from jax import lax
from jax.experimental import pallas as pl
from jax.experimental.pallas import tpu as pltpu

F32 = jnp.float32
BF16 = jnp.bfloat16

EPS = 1e-6
LOG2E = 1.4426950408889634

D_MODEL = 1024
D_FF = 2816
N_MOD = 9
GLA_HEADS = 4
GLA_DK = 512
GLA_DV = 1024
GLA_HK = GLA_DK // GLA_HEADS
GLA_HV = GLA_DV // GLA_HEADS
GLA_RANK = 16
GLA_TAU = 16.0
FOX_HEADS = 16
FOX_HD = 64
FOX_W = FOX_HEADS * FOX_HD
FOX_PAIRS = FOX_HEADS // 2

LANES = 128
VMEM_LIMIT = 56 * 1024 * 1024

FFN_TM = 512
FFN_CHUNKS = ((0, 1024), (1024, 1024), (2048, 768))
PROJ_TM = 256
GLA_CHUNK = 128
GLA_TT = 1024
GLA_LEVELS = (64, 32, 16, 8, 4, 2, 1)
FOX_BQ = 512
_FOX_SUM_ROWS = 16
MERGE_TM = 512

_C_QA, _C_KA, _C_VA, _C_RA = 0, 512, 1024, 2048
_C_QB, _C_KB, _C_VB, _C_GA, _C_GB = 3072, 4096, 5120, 6144, 7168
_C_SMALL = 8192
_W_COLS = 8320
_F_LANE0 = 16


def _resident(shape):
    nd = len(shape)
    return pl.BlockSpec(shape, lambda *_: (0,) * nd, pipeline_mode=pl.Buffered(1))


def _rms(x):
    return x * lax.rsqrt(jnp.mean(x * x, axis=-1, keepdims=True) + EPS)


def _log_sigmoid(x):
    return jnp.minimum(x, 0.0) - jnp.log(1.0 + jnp.exp(-jnp.abs(x)))


def _adaln_kernel(c_ref, w_ref, b_ref, o_ref):
    c = c_ref[...]
    s = c * jax.nn.sigmoid(c)
    o_ref[...] = jnp.dot(s.astype(BF16), w_ref[...].astype(BF16),
                         preferred_element_type=F32) + b_ref[...]


def _adaln(c_pad, w_ada, b_ada):
    rows = c_pad.shape[0]
    ncol = w_ada.shape[1]
    tn = 1024
    return pl.pallas_call(
        _adaln_kernel,
        out_shape=jax.ShapeDtypeStruct((rows, ncol), F32),
        grid=(ncol // tn,),
        in_specs=[pl.BlockSpec((rows, D_MODEL), lambda j: (0, 0)),
                  pl.BlockSpec((D_MODEL, tn), lambda j: (0, j)),
                  pl.BlockSpec((1, tn), lambda j: (0, j))],
        out_specs=pl.BlockSpec((rows, tn), lambda j: (0, j)),
        compiler_params=pltpu.CompilerParams(dimension_semantics=("arbitrary",)),
        name="adaln",
    )(c_pad, w_ada, b_ada)


def _ffn_kernel(x_ref, mod_ref, gpre_ref, gpost_ref, wgu_ref, wdn_ref, o_ref, *, mod0):
    x = x_ref[...]
    sh = mod_ref[0, mod0:mod0 + 1, :]
    sc = mod_ref[0, mod0 + 1:mod0 + 2, :]
    gt = mod_ref[0, mod0 + 2:mod0 + 3, :]
    h = (_rms(x) * gpre_ref[...]) * (1.0 + sc) + sh
    hb = h.astype(BF16)
    acc = None
    for c0, cw in FFN_CHUNKS:
        g = jnp.dot(hb, wgu_ref[:, c0:c0 + cw], preferred_element_type=F32)
        u = jnp.dot(hb, wgu_ref[:, D_FF + c0:D_FF + c0 + cw], preferred_element_type=F32)
        a = (g * jax.nn.sigmoid(g) * u).astype(BF16)
        part = jnp.dot(a, wdn_ref[c0:c0 + cw, :], preferred_element_type=F32)
        acc = part if acc is None else acc + part
    o_ref[...] = x + (0.5 * gt) * (_rms(acc) * gpost_ref[...])


def _ffn(x2d, mods, g_pre, g_post, w_gu, w_dn, *, mod0, seq):
    n = x2d.shape[0]
    tiles_per_batch = seq // FFN_TM
    return pl.pallas_call(
        functools.partial(_ffn_kernel, mod0=mod0),
        out_shape=jax.ShapeDtypeStruct((n, D_MODEL), F32),
        grid=(n // FFN_TM,),
        in_specs=[pl.BlockSpec((FFN_TM, D_MODEL), lambda i: (i, 0)),
                  pl.BlockSpec((1, N_MOD, D_MODEL), lambda i: (i // tiles_per_batch, 0, 0)),
                  _resident((1, D_MODEL)),
                  _resident((1, D_MODEL)),
                  _resident((D_MODEL, 2 * D_FF)),
                  _resident((D_FF, D_MODEL))],
        out_specs=pl.BlockSpec((FFN_TM, D_MODEL), lambda i: (i, 0)),
        compiler_params=pltpu.CompilerParams(dimension_semantics=("arbitrary",),
                                             vmem_limit_bytes=VMEM_LIMIT),
        name="ffn",
    )(x2d, mods, g_pre, g_post, w_gu, w_dn)


_S_ALOW = 2 * GLA_DK + GLA_DV
_S_WIDE1 = _S_ALOW + GLA_RANK
_S_FB = _S_WIDE1 + GLA_DV + 3 * FOX_W
_S_WIDE2 = _S_FB + FOX_HEADS
_S_END = _S_WIDE2 + 2 * D_MODEL
_REGROUP_TR = 128


def _regroup_kernel(w_ref, o_ref):
    o_ref[:, 0:_S_ALOW] = w_ref[:, 0:_S_ALOW].astype(BF16)
    o_ref[:, _C_RA:_C_GA] = w_ref[:, _S_WIDE1:_S_FB].astype(BF16)
    o_ref[:, _C_GA:_C_SMALL] = w_ref[:, _S_WIDE2:_S_END].astype(BF16)
    a_blk = w_ref[:, _S_ALOW:_S_ALOW + LANES]
    f_blk = w_ref[:, _S_FB - _F_LANE0:_S_FB - _F_LANE0 + LANES]
    lane = lax.broadcasted_iota(jnp.int32, a_blk.shape, 1)
    small = jnp.where(lane < GLA_RANK, a_blk,
                      jnp.where(lane < _F_LANE0 + FOX_HEADS, f_blk, 0.0))
    o_ref[:, _C_SMALL:_W_COLS] = small.astype(BF16)


def _regroup(w_in):
    rows, cols = w_in.shape
    assert cols == _S_END and (_S_FB - _F_LANE0) % LANES == 0 and _F_LANE0 == GLA_RANK
    return pl.pallas_call(
        _regroup_kernel,
        out_shape=jax.ShapeDtypeStruct((rows, _W_COLS), BF16),
        grid=(rows // _REGROUP_TR,),
        in_specs=[pl.BlockSpec((_REGROUP_TR, cols), lambda i: (i, 0))],
        out_specs=pl.BlockSpec((_REGROUP_TR, _W_COLS), lambda i: (i, 0)),
        compiler_params=pltpu.CompilerParams(dimension_semantics=("arbitrary",)),
        name="regroup",
    )(w_in)


def _proj_kernel(x_ref, mod_ref, gpre_ref, w_ref, wa2_ref, ba_ref, bsm_ref, eqk_ref,
                 qa_ref, ka_ref, va_ref, la_ref, ra_ref,
                 qe_ref, qo_ref, ke_ref, ko_ref, vt_ref, sga_ref, sgb_ref,
                 fcar_ref, *, tiles_per_batch):
    i = pl.program_id(0)
    tm = x_ref.shape[0]

    @pl.when(i % tiles_per_batch == 0)
    def _():
        fcar_ref[...] = jnp.zeros_like(fcar_ref)

    x = x_ref[...]
    sh = mod_ref[0, 3:4, :]
    sc = mod_ref[0, 4:5, :]
    hb = ((_rms(x) * gpre_ref[...]) * (1.0 + sc) + sh).astype(BF16)

    def proj(c0, cw):
        return jnp.dot(hb, w_ref[:, c0:c0 + cw], preferred_element_type=F32)

    qa_ref[...] = proj(_C_QA, GLA_DK) * (GLA_HK ** -0.5)
    ka_ref[...] = proj(_C_KA, GLA_DK)
    va_ref[...] = proj(_C_VA, GLA_DV).astype(BF16)
    r = proj(_C_RA, GLA_DV)
    ra_ref[...] = (r * jax.nn.sigmoid(r)).astype(BF16)

    zs = proj(_C_SMALL, LANES)
    xa = jnp.dot(zs.astype(BF16), wa2_ref[...], preferred_element_type=F32) + ba_ref[...]
    la_ref[...] = _log_sigmoid(xa) * (1.0 / GLA_TAU)

    lane = lax.broadcasted_iota(jnp.int32, (tm, LANES), 1)
    row = lax.broadcasted_iota(jnp.int32, (tm, LANES), 0)
    in_f = (lane >= _F_LANE0) & (lane < _F_LANE0 + FOX_HEADS)
    f = jnp.where(in_f, _log_sigmoid(zs + bsm_ref[...]), 0.0)
    shift = 1
    while shift < tm:
        f = f + jnp.where(row >= shift, pltpu.roll(f, shift, 0), 0.0)
        shift *= 2
    f = f + fcar_ref[...]
    fcar_ref[...] = f[tm - 1:tm, :]

    f2 = f * LOG2E
    p0 = f2.astype(BF16).astype(F32)
    r1 = f2 - p0
    p1 = r1.astype(BF16).astype(F32)
    p2 = (r1 - p1).astype(BF16).astype(F32)
    fc = p0 + pltpu.roll(p1, FOX_HEADS, 1) + pltpu.roll(p2, 2 * FOX_HEADS, 1)
    fc = jnp.where(lane == 0, 1.0, fc)
    aug = jnp.dot(fc.astype(BF16), eqk_ref[...], preferred_element_type=F32)
    augq = aug[:, :FOX_W]
    augk = aug[:, FOX_W:]

    lane_w = lax.broadcasted_iota(jnp.int32, (tm, FOX_W), 1)
    low_half = (lane_w & (LANES - 1)) < FOX_HD
    zq = proj(_C_QB, FOX_W) * (FOX_HD ** -0.5 * LOG2E)
    q_even = jnp.where(low_half, zq, augq)
    q_odd = jnp.where(low_half, augq, zq)
    for j in range(FOX_PAIRS):
        qe_ref[0, j, 0] = q_even[:, j * LANES:(j + 1) * LANES].T.astype(BF16)
        qo_ref[0, j, 0] = q_odd[:, j * LANES:(j + 1) * LANES].T.astype(BF16)
    zk = proj(_C_KB, FOX_W)
    ke_ref[...] = jnp.where(low_half, zk, augk).astype(BF16)
    ko_ref[...] = jnp.where(low_half, augk, zk).astype(BF16)
    zv = proj(_C_VB, FOX_W)
    for j in range(FOX_PAIRS):
        vt_ref[0, j, 0] = zv[:, j * LANES:(j + 1) * LANES].T.astype(BF16)
    sga_ref[...] = jax.nn.sigmoid(proj(_C_GA, D_MODEL)).astype(BF16)
    sgb_ref[...] = jax.nn.sigmoid(proj(_C_GB, D_MODEL)).astype(BF16)


def _proj(x2d, mods, g_pre, w_all, w_a2p, b_a, b_small, e_qk, *, batch, seq):
    n = x2d.shape[0]
    tm = PROJ_TM
    tiles_per_batch = seq // tm
    row_spec = lambda w: pl.BlockSpec((tm, w), lambda i: (i, 0))
    transposed = jax.ShapeDtypeStruct((batch, FOX_PAIRS, tiles_per_batch, LANES, tm), BF16)
    out_shapes = [
        jax.ShapeDtypeStruct((n, GLA_DK), F32),
        jax.ShapeDtypeStruct((n, GLA_DK), F32),
        jax.ShapeDtypeStruct((n, GLA_DV), BF16),
        jax.ShapeDtypeStruct((n, GLA_DK), F32),
        jax.ShapeDtypeStruct((n, GLA_DV), BF16),
        transposed,
        transposed,
        jax.ShapeDtypeStruct((n, FOX_W), BF16),
        jax.ShapeDtypeStruct((n, FOX_W), BF16),
        transposed,
        jax.ShapeDtypeStruct((n, D_MODEL), BF16),
        jax.ShapeDtypeStruct((n, D_MODEL), BF16),
    ]
    return pl.pallas_call(
        functools.partial(_proj_kernel, tiles_per_batch=tiles_per_batch),
        out_shape=out_shapes,
        grid=(n // tm,),
        in_specs=[row_spec(D_MODEL),
                  pl.BlockSpec((1, N_MOD, D_MODEL), lambda i: (i // tiles_per_batch, 0, 0)),
                  _resident((1, D_MODEL)),
                  _resident((D_MODEL, _W_COLS)),
                  _resident((LANES, GLA_DK)),
                  _resident((1, GLA_DK)),
                  _resident((1, LANES)),
                  _resident((LANES, 2 * FOX_W))],
        out_specs=[row_spec(s.shape[1]) if len(s.shape) == 2 else
                   pl.BlockSpec((1, FOX_PAIRS, 1, LANES, tm),
                                lambda i: (i // tiles_per_batch, 0, i % tiles_per_batch, 0, 0))
                   for s in out_shapes],
        scratch_shapes=[pltpu.VMEM((1, LANES), F32)],
        compiler_params=pltpu.CompilerParams(dimension_semantics=("arbitrary",),
                                             vmem_limit_bytes=VMEM_LIMIT),
        name="proj",
    )(x2d, mods, g_pre, w_all, w_a2p, b_a, b_small, e_qk)


def _gla_tables():
    c = GLA_CHUNK
    t = np.arange(c)[:, None]
    u = np.arange(c)[None, :]
    masks = [(t == u)]
    for s in GLA_LEVELS:
        same = (t // (2 * s)) == (u // (2 * s))
        masks.append(same & (t % (2 * s) >= s) & (u % (2 * s) < s))
    ltri = (u <= t).astype(np.float32)
    mask = np.stack(masks, axis=0).astype(np.float32)
    return ltri, mask


def _dot_nt(a, b):
    return lax.dot_general(a, b, (((1,), (1,)), ((), ())), preferred_element_type=F32)


def _gla_pivots(b, b_rows_ref, s):
    c, dk = b.shape
    if s >= 8:
        return jnp.concatenate(
            [jnp.broadcast_to(b_rows_ref[pl.ds(g0 + s, 1), :], (2 * s, dk))
             for g0 in range(0, c, 2 * s)], axis=0)
    b3 = b.reshape(c // 8, 8, dk)
    sub = lax.broadcasted_iota(jnp.int32, b3.shape, 1)
    offset = s - (sub & (2 * s - 1))
    piv = b3
    for d in range(-(s - 1), s + 1):
        if d != 0:
            piv = jnp.where(offset == d, pltpu.roll(b3, (-d) % 8, 1), piv)
    return piv.reshape(c, dk)


def _gla_kernel(q_ref, k_ref, v_ref, g_ref, r_ref, gg_ref, ltri_ref, mask_ref, o_ref, s_ref, b_ref):
    c = GLA_CHUNK

    @pl.when(pl.program_id(2) == 0)
    def _():
        s_ref[...] = jnp.zeros_like(s_ref)

    state = s_ref[...]
    for ci in range(q_ref.shape[0] // c):
        sl = pl.ds(ci * c, c)
        q = q_ref[sl, :]
        k = k_ref[sl, :]
        g = g_ref[sl, :]
        v = v_ref[sl, :]
        g_hi = g.astype(BF16)
        g_lo = (g - g_hi.astype(F32)).astype(BF16)
        x2 = jnp.dot(ltri_ref[...], jnp.concatenate([g_hi, g_lo], axis=1),
                     preferred_element_type=F32)
        b = (x2[:, :GLA_HK] + x2[:, GLA_HK:]) * LOG2E
        b_ref[ci] = b
        eb = jnp.exp2(b)
        ebl = jnp.exp2(b_ref[ci, pl.ds(c - 1, 1), :] - b)

        a = mask_ref[0] * _dot_nt(q.astype(BF16), k.astype(BF16))
        for li, s in enumerate(GLA_LEVELS):
            f = jnp.exp2(-jnp.abs(b - _gla_pivots(b, b_ref.at[ci], s)))
            a = a + mask_ref[li + 1] * _dot_nt((q * f).astype(BF16), (k * f).astype(BF16))

        lhs = jnp.concatenate([(q * eb).astype(BF16), a.astype(BF16)], axis=1)
        rhs = jnp.concatenate([state.astype(BF16), v], axis=0)
        o = jnp.dot(lhs, rhs, preferred_element_type=F32)

        upd = jnp.dot((k * ebl).T.astype(BF16), v, preferred_element_type=F32)
        decay = eb.T[:, c - 1:c]
        state = decay * state + upd

        on = _rms(o) * gg_ref[0]
        o_ref[sl, :] = (on * r_ref[sl, :].astype(F32)).astype(BF16)
    s_ref[...] = state


def _gla(qa, ka, va, la, ra, g_gla, ltri, mask, *, batch, seq):
    n = qa.shape[0]
    tiles = seq // GLA_TT
    rows = lambda b, h, t: b * tiles + t
    return pl.pallas_call(
        _gla_kernel,
        out_shape=jax.ShapeDtypeStruct((n, GLA_DV), BF16),
        grid=(batch, GLA_HEADS, tiles),
        in_specs=[pl.BlockSpec((GLA_TT, GLA_HK), lambda b, h, t: (rows(b, h, t), h)),
                  pl.BlockSpec((GLA_TT, GLA_HK), lambda b, h, t: (rows(b, h, t), h)),
                  pl.BlockSpec((GLA_TT, GLA_HV), lambda b, h, t: (rows(b, h, t), h)),
                  pl.BlockSpec((GLA_TT, GLA_HK), lambda b, h, t: (rows(b, h, t), h)),
                  pl.BlockSpec((GLA_TT, GLA_HV), lambda b, h, t: (rows(b, h, t), h)),
                  pl.BlockSpec((1, 1, GLA_HV), lambda b, h, t: (h, 0, 0)),
                  _resident(ltri.shape),
                  _resident(mask.shape)],
        out_specs=pl.BlockSpec((GLA_TT, GLA_HV), lambda b, h, t: (rows(b, h, t), h)),
        scratch_shapes=[pltpu.VMEM((GLA_HK, GLA_HV), F32),
                        pltpu.VMEM((GLA_TT // GLA_CHUNK, GLA_CHUNK, GLA_HK), F32)],
        compiler_params=pltpu.CompilerParams(
            dimension_semantics=("arbitrary", "arbitrary", "arbitrary"),
            vmem_limit_bytes=VMEM_LIMIT),
        name="gla",
    )(qa, ka, va, la, ra, g_gla, ltri, mask)


def _fox_kernel(qe_ref, qo_ref, qen_ref, qon_ref, ke_ref, ko_ref, vt_ref, o_ref,
                m_ref, acc_ref, s0_ref, sa_ref, sb_ref):
    i = pl.program_id(2)
    n_vt = qe_ref.shape[2]
    bq = n_vt * qe_ref.shape[-1]
    bk = bq
    q_cur = (qe_ref, qo_ref)
    q_next = (qen_ref, qon_ref)
    k_refs = (ke_ref, ko_ref)

    m_ref[...] = jnp.full_like(m_ref, -jnp.inf)
    acc_ref[...] = jnp.zeros_like(acc_ref)

    def logits(dst_ref, t, q_refs=q_cur):
        ks = pl.ds(pl.multiple_of(t * bk, bk), bk)
        for hd in range(2):
            qt = jnp.concatenate([q_refs[hd][0, 0, c] for c in range(n_vt)], axis=1)
            dst_ref[hd] = jnp.dot(k_refs[hd][ks, :], qt, preferred_element_type=F32)

    def consume(src_ref, t, masked):
        vt = jnp.concatenate([vt_ref[0, 0, t * n_vt + c] for c in range(n_vt)], axis=1)
        ones = jnp.ones((_FOX_SUM_ROWS, bk), BF16)
        for hd in range(2):
            vaug = jnp.concatenate([vt[hd * FOX_HD:(hd + 1) * FOX_HD], ones], axis=0)
            s = src_ref[hd]
            if masked:
                k_i = lax.broadcasted_iota(jnp.int32, (bk, bq), 0)
                q_i = lax.broadcasted_iota(jnp.int32, (bk, bq), 1)
                s = jnp.where(k_i <= q_i, s, -jnp.inf)
            m_old = m_ref[hd]
            m_new = jnp.maximum(m_old, jnp.max(s, axis=0, keepdims=True))
            p = jnp.exp2(s - m_new).astype(BF16)
            acc_ref[hd] = jnp.exp2(m_old - m_new) * acc_ref[hd] + jnp.dot(
                vaug, p, preferred_element_type=F32)
            m_ref[hd] = m_new

    def pair(u, carry):
        logits(sb_ref, 2 * u + 2)
        consume(sa_ref, 2 * u + 1, False)
        logits(sa_ref, 2 * u + 3)
        consume(sb_ref, 2 * u + 2, False)
        return carry

    def head_and_pairs():
        logits(sa_ref, 1)
        consume(s0_ref, 0, False)
        lax.fori_loop(0, (i - 1) // 2, pair, 0)

    @pl.when(i == 0)
    def _():
        logits(s0_ref, 0)
        consume(s0_ref, 0, True)
        logits(s0_ref, 0, q_next)

    @pl.when(i % 2 == 1)
    def _():
        head_and_pairs()
        logits(s0_ref, 0, q_next)
        consume(sa_ref, i, True)

    @pl.when((i > 0) & (i % 2 == 0))
    def _():
        head_and_pairs()
        logits(sb_ref, i)
        consume(sa_ref, i - 1, False)
        logits(s0_ref, 0, q_next)
        consume(sb_ref, i, True)

    oe = acc_ref[0]
    oo = acc_ref[1]
    ot = jnp.concatenate([oe[:FOX_HD] / oe[FOX_HD:FOX_HD + 1],
                          oo[:FOX_HD] / oo[FOX_HD:FOX_HD + 1]], axis=0)
    o_ref[...] = ot.T.astype(BF16)


def _fox(qe, qo, ke, ko, vt, *, batch, seq):
    n = batch * seq
    bq = FOX_BQ
    tiles = seq // bq
    q_blk = (1, 1, bq // vt.shape[-1]) + vt.shape[3:]
    q_spec = pl.BlockSpec(q_blk, lambda b, j, i: (b, j, i, 0, 0))
    q_next_spec = pl.BlockSpec(q_blk, lambda b, j, i: (b, j, jnp.minimum(i + 1, tiles - 1), 0, 0))
    kv_spec = pl.BlockSpec((seq, LANES), lambda b, j, i: (b, j))
    logit_buf = pltpu.VMEM((2, bq, bq), F32)
    return pl.pallas_call(
        _fox_kernel,
        out_shape=jax.ShapeDtypeStruct((n, FOX_W), BF16),
        grid=(batch, FOX_PAIRS, tiles),
        in_specs=[q_spec, q_spec, q_next_spec, q_next_spec, kv_spec, kv_spec,
                  pl.BlockSpec((1, 1) + vt.shape[2:], lambda b, j, i: (b, j, 0, 0, 0))],
        out_specs=pl.BlockSpec((bq, LANES), lambda b, j, i: (b * tiles + i, j)),
        scratch_shapes=[pltpu.VMEM((2, 1, bq), F32),
                        pltpu.VMEM((2, FOX_HD + _FOX_SUM_ROWS, bq), F32),
                        logit_buf, logit_buf, logit_buf],
        compiler_params=pltpu.CompilerParams(
            dimension_semantics=("arbitrary", "arbitrary", "arbitrary"),
            vmem_limit_bytes=VMEM_LIMIT),
        name="fox",
    )(qe, qo, qe, qo, ke, ko, vt)


def _merge_kernel(x_ref, mod_ref, gpost_ref, oa_ref, ob_ref, sga_ref, sgb_ref,
                  wpa_ref, wpb_ref, wout_ref, o_ref):
    pa = jnp.dot(oa_ref[...], wpa_ref[...], preferred_element_type=F32)
    pb = jnp.dot(ob_ref[...], wpb_ref[...], preferred_element_type=F32)
    merged = sga_ref[...].astype(F32) * pa + sgb_ref[...].astype(F32) * pb
    y = jnp.dot(merged.astype(BF16), wout_ref[...], preferred_element_type=F32)
    gt = mod_ref[0, 5:6, :]
    o_ref[...] = x_ref[...] + gt * (_rms(y) * gpost_ref[...])


def _merge(x2d, mods, g_post, oa, ob, sga, sgb, w_pa, w_pb, w_out, *, seq):
    n = x2d.shape[0]
    tm = MERGE_TM
    tiles_per_batch = seq // tm
    row_spec = pl.BlockSpec((tm, D_MODEL), lambda i: (i, 0))
    return pl.pallas_call(
        _merge_kernel,
        out_shape=jax.ShapeDtypeStruct((n, D_MODEL), F32),
        grid=(n // tm,),
        in_specs=[row_spec,
                  pl.BlockSpec((1, N_MOD, D_MODEL), lambda i: (i // tiles_per_batch, 0, 0)),
                  _resident((1, D_MODEL)),
                  row_spec, row_spec, row_spec, row_spec,
                  _resident((D_MODEL, D_MODEL)),
                  _resident((D_MODEL, D_MODEL)),
                  _resident((D_MODEL, D_MODEL))],
        out_specs=row_spec,
        compiler_params=pltpu.CompilerParams(dimension_semantics=("arbitrary",),
                                             vmem_limit_bytes=VMEM_LIMIT),
        name="merge",
    )(x2d, mods, g_post, oa, ob, sga, sgb, w_pa, w_pb, w_out)


def _fox_bias_placement():
    e = np.zeros((LANES, 2 * FOX_W), np.float32)
    for h in range(FOX_HEADS):
        base = LANES * (h // 2) + (FOX_HD if h % 2 == 0 else 0)
        for p in range(3):
            src = _F_LANE0 + FOX_HEADS * p + h
            e[src, base + p] = 1.0
            e[0, base + 3 + p] = 1.0
            e[0, FOX_W + base + p] = 1.0
            e[src, FOX_W + base + 3 + p] = -1.0
    return e


def kernel(x, c, w_ada, b_ada, g_pre, g_post, w_gu1, w_dn1, w_gu2, w_dn2,
           w_in, w_a2, b_a, b_f, g_gla, w_pa, w_pb, w_out):
    batch, seq, d = x.shape
    n = batch * seq
    depth = w_ada.shape[0]
    ltri_np, mask_np = _gla_tables()
    ltri = jnp.asarray(ltri_np, BF16)
    mask = jnp.asarray(mask_np, F32)
    e_qk = jnp.asarray(_fox_bias_placement(), BF16)

    x2d = x.reshape(n, d)
    c_pad = jnp.pad(c, ((0, 16 - batch), (0, 0)))
    for l in range(depth):
        mods = _adaln(c_pad, w_ada[l], b_ada[l][None, :])[:batch].reshape(batch, N_MOD, d)

        x2d = _ffn(x2d, mods, g_pre[l, 0][None, :], g_post[l, 0][None, :],
                   w_gu1[l].astype(BF16), w_dn1[l].astype(BF16), mod0=0, seq=seq)

        w_all = _regroup(w_in[l])
        w_a2p = jnp.pad(w_a2[l], ((0, LANES - GLA_RANK), (0, 0))).astype(BF16)
        b_small = jnp.pad(b_f[l], (_F_LANE0, LANES - _F_LANE0 - FOX_HEADS))[None, :]

        (qa, ka, va, la, ra, qe, qo, ke, ko, vt, sga, sgb) = _proj(
            x2d, mods, g_pre[l, 1][None, :], w_all, w_a2p, b_a[l][None, :], b_small, e_qk,
            batch=batch, seq=seq)

        oa = _gla(qa, ka, va, la, ra, g_gla[l].reshape(GLA_HEADS, 1, GLA_HV), ltri, mask,
                  batch=batch, seq=seq)
        ob = _fox(qe, qo, ke, ko, vt, batch=batch, seq=seq)

        x2d = _merge(x2d, mods, g_post[l, 1][None, :], oa, ob, sga, sgb,
                     w_pa[l].astype(BF16), w_pb[l].astype(BF16), w_out[l].astype(BF16), seq=seq)

        x2d = _ffn(x2d, mods, g_pre[l, 2][None, :], g_post[l, 2][None, :],
                   w_gu2[l].astype(BF16), w_dn2[l].astype(BF16), mod0=6, seq=seq)
    return x2d.reshape(batch, seq, d)
```

```python
import functools

import numpy as np
import jax
import jax.numpy as jnp
from jax import lax
from jax.experimental import pallas as pl
from jax.experimental.pallas import tpu as pltpu

F32 = jnp.float32
BF16 = jnp.bfloat16

EPS = 1e-6
LOG2E = 1.4426950408889634

D_MODEL = 1024
D_FF = 2816
N_MOD = 9
GLA_HEADS = 4
GLA_DK = 512
GLA_DV = 1024
GLA_HK = GLA_DK // GLA_HEADS
GLA_HV = GLA_DV // GLA_HEADS
GLA_RANK = 16
GLA_TAU = 16.0
FOX_HEADS = 16
FOX_HD = 64
FOX_W = FOX_HEADS * FOX_HD
FOX_PAIRS = FOX_HEADS // 2

LANES = 128
VMEM_LIMIT = 56 * 1024 * 1024

FFN_TM = 512
FFN_CHUNKS = ((0, 1024), (1024, 1024), (2048, 768))
PROJ_TM = 256
GLA_CHUNK = 128
GLA_TT = 1024
GLA_HPS = 1
GLA_LEVELS = (64, 32, 16, 8, 4, 2, 1)
FOX_BQ = 512
_FOX_SUM_ROWS = 16
_FOX_QSUB = 256
MERGE_TM = 512

_C_QA, _C_KA, _C_VA, _C_RA = 0, 512, 1024, 2048
_C_QB, _C_KB, _C_VB, _C_GA, _C_GB = 3072, 4096, 5120, 6144, 7168
_C_SMALL = 8192
_W_COLS = 8320
_F_LANE0 = 16


def _resident(shape):
    nd = len(shape)
    return pl.BlockSpec(shape, lambda *_: (0,) * nd, pipeline_mode=pl.Buffered(1))


def _rms(x):
    return x * lax.rsqrt(jnp.mean(x * x, axis=-1, keepdims=True) + EPS)


def _log_sigmoid(x):
    return jnp.minimum(x, 0.0) - jnp.log(1.0 + jnp.exp(-jnp.abs(x)))


def _adaln_kernel(c_ref, w_ref, b_ref, o_ref):
    c = c_ref[...]
    s = c * jax.nn.sigmoid(c)
    o_ref[...] = jnp.dot(s.astype(BF16), w_ref[...].astype(BF16),
                         preferred_element_type=F32) + b_ref[...]


def _adaln(c_pad, w_ada, b_ada):
    rows = c_pad.shape[0]
    ncol = w_ada.shape[1]
    tn = 1024
    return pl.pallas_call(
        _adaln_kernel,
        out_shape=jax.ShapeDtypeStruct((rows, ncol), F32),
        grid=(ncol // tn,),
        in_specs=[pl.BlockSpec((rows, D_MODEL), lambda j: (0, 0)),
                  pl.BlockSpec((D_MODEL, tn), lambda j: (0, j)),
                  pl.BlockSpec((1, tn), lambda j: (0, j))],
        out_specs=pl.BlockSpec((rows, tn), lambda j: (0, j)),
        compiler_params=pltpu.CompilerParams(dimension_semantics=("arbitrary",)),
        name="adaln",
    )(c_pad, w_ada, b_ada)


def _ffn_kernel(x_ref, mod_ref, gpre_ref, gpost_ref, wgu_ref, wdn_ref, o_ref, *, mod0):
    x = x_ref[...]
    sh = mod_ref[0, mod0:mod0 + 1, :]
    sc = mod_ref[0, mod0 + 1:mod0 + 2, :]
    gt = mod_ref[0, mod0 + 2:mod0 + 3, :]
    h = (_rms(x) * gpre_ref[...]) * (1.0 + sc) + sh
    hb = h.astype(BF16)
    acc = None
    for c0, cw in FFN_CHUNKS:
        g = jnp.dot(hb, wgu_ref[:, c0:c0 + cw], preferred_element_type=F32)
        u = jnp.dot(hb, wgu_ref[:, D_FF + c0:D_FF + c0 + cw], preferred_element_type=F32)
        a = (g * jax.nn.sigmoid(g) * u).astype(BF16)
        part = jnp.dot(a, wdn_ref[c0:c0 + cw, :], preferred_element_type=F32)
        acc = part if acc is None else acc + part
    o_ref[...] = x + (0.5 * gt) * (_rms(acc) * gpost_ref[...])


def _ffn(x2d, mods, g_pre, g_post, w_gu, w_dn, *, mod0, seq):
    n = x2d.shape[0]
    tiles_per_batch = seq // FFN_TM
    return pl.pallas_call(
        functools.partial(_ffn_kernel, mod0=mod0),
        out_shape=jax.ShapeDtypeStruct((n, D_MODEL), F32),
        grid=(n // FFN_TM,),
        in_specs=[pl.BlockSpec((FFN_TM, D_MODEL), lambda i: (i, 0)),
                  pl.BlockSpec((1, N_MOD, D_MODEL), lambda i: (i // tiles_per_batch, 0, 0)),
                  _resident((1, D_MODEL)),
                  _resident((1, D_MODEL)),
                  _resident((D_MODEL, 2 * D_FF)),
                  _resident((D_FF, D_MODEL))],
        out_specs=pl.BlockSpec((FFN_TM, D_MODEL), lambda i: (i, 0)),
        compiler_params=pltpu.CompilerParams(dimension_semantics=("arbitrary",),
                                             vmem_limit_bytes=VMEM_LIMIT),
        name="ffn",
    )(x2d, mods, g_pre, g_post, w_gu, w_dn)


_S_ALOW = 2 * GLA_DK + GLA_DV
_S_WIDE1 = _S_ALOW + GLA_RANK
_S_FB = _S_WIDE1 + GLA_DV + 3 * FOX_W
_S_WIDE2 = _S_FB + FOX_HEADS
_S_END = _S_WIDE2 + 2 * D_MODEL
_REGROUP_TR = 128


def _regroup_kernel(wt_ref, o_ref):
    def blk(src):
        return wt_ref[src:src + LANES, :].T

    def put(dst, src, width):
        for c in range(0, width, LANES):
            o_ref[:, dst + c:dst + c + LANES] = blk(src + c).astype(BF16)

    put(_C_QA, 0, _S_ALOW)
    put(_C_RA, _S_WIDE1, _S_FB - _S_WIDE1)
    put(_C_GA, _S_WIDE2, _S_END - _S_WIDE2)
    a_blk = blk(_S_ALOW)
    f_blk = blk(_S_FB - _F_LANE0)
    lane = lax.broadcasted_iota(jnp.int32, a_blk.shape, 1)
    small = jnp.where(lane < GLA_RANK, a_blk,
                      jnp.where(lane < _F_LANE0 + FOX_HEADS, f_blk, 0.0))
    o_ref[:, _C_SMALL:_W_COLS] = small.astype(BF16)


def _regroup(w_in_t):
    cols, rows = w_in_t.shape
    assert cols == _S_END and (_S_FB - _F_LANE0) % LANES == 0 and _F_LANE0 == GLA_RANK
    return pl.pallas_call(
        _regroup_kernel,
        out_shape=jax.ShapeDtypeStruct((rows, _W_COLS), BF16),
        grid=(rows // _REGROUP_TR,),
        in_specs=[pl.BlockSpec((cols, _REGROUP_TR), lambda i: (0, i))],
        out_specs=pl.BlockSpec((_REGROUP_TR, _W_COLS), lambda i: (i, 0)),
        compiler_params=pltpu.CompilerParams(dimension_semantics=("arbitrary",)),
        name="regroup",
    )(w_in_t)


def _proj_kernel(x_ref, mod_ref, gpre_ref, w_ref, wa2_ref, ba_ref, bsm_ref, eqk_ref,
                 qa_ref, ka_ref, va_ref, la_ref, ra_ref,
                 qe_ref, qo_ref, ke_ref, ko_ref, vt_ref, sga_ref, sgb_ref,
                 fcar_ref, *, tiles_per_batch):
    i = pl.program_id(0)
    tm = x_ref.shape[0]

    @pl.when(i % tiles_per_batch == 0)
    def _():
        fcar_ref[...] = jnp.zeros_like(fcar_ref)

    x = x_ref[...]
    sh = mod_ref[0, 3:4, :]
    sc = mod_ref[0, 4:5, :]
    hb = ((_rms(x) * gpre_ref[...]) * (1.0 + sc) + sh).astype(BF16)

    def proj(c0, cw):
        return jnp.dot(hb, w_ref[:, c0:c0 + cw], preferred_element_type=F32)

    qa_ref[...] = proj(_C_QA, GLA_DK) * (GLA_HK ** -0.5)
    ka_ref[...] = proj(_C_KA, GLA_DK)
    va_ref[...] = proj(_C_VA, GLA_DV).astype(BF16)
    r = proj(_C_RA, GLA_DV)
    ra_ref[...] = (r * jax.nn.sigmoid(r)).astype(BF16)

    zs = proj(_C_SMALL, LANES)
    xa = jnp.dot(zs.astype(BF16), wa2_ref[...], preferred_element_type=F32) + ba_ref[...]
    la_ref[...] = _log_sigmoid(xa) * (1.0 / GLA_TAU)

    lane = lax.broadcasted_iota(jnp.int32, (tm, LANES), 1)
    row = lax.broadcasted_iota(jnp.int32, (tm, LANES), 0)
    in_f = (lane >= _F_LANE0) & (lane < _F_LANE0 + FOX_HEADS)
    f = jnp.where(in_f, _log_sigmoid(zs + bsm_ref[...]), 0.0)
    shift = 1
    while shift < tm:
        f = f + jnp.where(row >= shift, pltpu.roll(f, shift, 0), 0.0)
        shift *= 2
    f = f + fcar_ref[...]
    fcar_ref[...] = f[tm - 1:tm, :]

    f2 = f * LOG2E
    p0 = f2.astype(BF16).astype(F32)
    r1 = f2 - p0
    p1 = r1.astype(BF16).astype(F32)
    p2 = (r1 - p1).astype(BF16).astype(F32)
    fc = p0 + pltpu.roll(p1, FOX_HEADS, 1) + pltpu.roll(p2, 2 * FOX_HEADS, 1)
    fc = jnp.where(lane == 0, 1.0, fc)
    aug = jnp.dot(fc.astype(BF16), eqk_ref[...], preferred_element_type=F32)
    augq = aug[:, :FOX_W]
    augk = aug[:, FOX_W:]

    lane_w = lax.broadcasted_iota(jnp.int32, (tm, FOX_W), 1)
    low_half = (lane_w & (LANES - 1)) < FOX_HD
    zq = proj(_C_QB, FOX_W) * (FOX_HD ** -0.5 * LOG2E)
    q_even = jnp.where(low_half, zq, augq)
    q_odd = jnp.where(low_half, augq, zq)
    for j in range(FOX_PAIRS):
        qe_ref[0, j, 0] = q_even[:, j * LANES:(j + 1) * LANES].T.astype(BF16)
        qo_ref[0, j, 0] = q_odd[:, j * LANES:(j + 1) * LANES].T.astype(BF16)
    zk = proj(_C_KB, FOX_W)
    ke_ref[...] = jnp.where(low_half, zk, augk).astype(BF16)
    ko_ref[...] = jnp.where(low_half, augk, zk).astype(BF16)
    zv = proj(_C_VB, FOX_W)
    for j in range(FOX_PAIRS):
        vt_ref[0, j, 0] = zv[:, j * LANES:(j + 1) * LANES].T.astype(BF16)
    sga_ref[...] = jax.nn.sigmoid(proj(_C_GA, D_MODEL)).astype(BF16)
    sgb_ref[...] = jax.nn.sigmoid(proj(_C_GB, D_MODEL)).astype(BF16)


def _proj(x2d, mods, g_pre, w_all, w_a2p, b_a, b_small, e_qk, *, batch, seq):
    n = x2d.shape[0]
    tm = PROJ_TM
    tiles_per_batch = seq // tm
    row_spec = lambda w: pl.BlockSpec((tm, w), lambda i: (i, 0))
    transposed = jax.ShapeDtypeStruct((batch, FOX_PAIRS, tiles_per_batch, LANES, tm), BF16)
    out_shapes = [
        jax.ShapeDtypeStruct((n, GLA_DK), F32),
        jax.ShapeDtypeStruct((n, GLA_DK), F32),
        jax.ShapeDtypeStruct((n, GLA_DV), BF16),
        jax.ShapeDtypeStruct((n, GLA_DK), F32),
        jax.ShapeDtypeStruct((n, GLA_DV), BF16),
        transposed,
        transposed,
        jax.ShapeDtypeStruct((n, FOX_W), BF16),
        jax.ShapeDtypeStruct((n, FOX_W), BF16),
        transposed,
        jax.ShapeDtypeStruct((n, D_MODEL), BF16),
        jax.ShapeDtypeStruct((n, D_MODEL), BF16),
    ]
    return pl.pallas_call(
        functools.partial(_proj_kernel, tiles_per_batch=tiles_per_batch),
        out_shape=out_shapes,
        grid=(n // tm,),
        in_specs=[row_spec(D_MODEL),
                  pl.BlockSpec((1, N_MOD, D_MODEL), lambda i: (i // tiles_per_batch, 0, 0)),
                  _resident((1, D_MODEL)),
                  _resident((D_MODEL, _W_COLS)),
                  _resident((LANES, GLA_DK)),
                  _resident((1, GLA_DK)),
                  _resident((1, LANES)),
                  _resident((LANES, 2 * FOX_W))],
        out_specs=[row_spec(s.shape[1]) if len(s.shape) == 2 else
                   pl.BlockSpec((1, FOX_PAIRS, 1, LANES, tm),
                                lambda i: (i // tiles_per_batch, 0, i % tiles_per_batch, 0, 0))
                   for s in out_shapes],
        scratch_shapes=[pltpu.VMEM((1, LANES), F32)],
        compiler_params=pltpu.CompilerParams(dimension_semantics=("arbitrary",),
                                             vmem_limit_bytes=VMEM_LIMIT),
        name="proj",
    )(x2d, mods, g_pre, w_all, w_a2p, b_a, b_small, e_qk)


def _gla_tables():
    c = GLA_CHUNK
    t = np.arange(c)[:, None]
    u = np.arange(c)[None, :]
    masks = [(t == u)]
    for s in GLA_LEVELS:
        same = (t // (2 * s)) == (u // (2 * s))
        masks.append(same & (t % (2 * s) >= s) & (u % (2 * s) < s))
    ltri = (u <= t).astype(np.float32)
    mask = np.stack(masks, axis=0).astype(np.float32)
    return ltri, mask


def _dot_nt(a, b):
    return lax.dot_general(a, b, (((1,), (1,)), ((), ())), preferred_element_type=F32)


def _gla_pivots(b, b_rows_ref, s):
    c, dk = b.shape
    if s >= 8:
        return jnp.concatenate(
            [jnp.broadcast_to(b_rows_ref[pl.ds(g0 + s, 1), :], (2 * s, dk))
             for g0 in range(0, c, 2 * s)], axis=0)
    b3 = b.reshape(c // 8, 8, dk)
    sub = lax.broadcasted_iota(jnp.int32, b3.shape, 1)
    offset = s - (sub & (2 * s - 1))
    piv = b3
    for d in range(-(s - 1), s + 1):
        if d != 0:
            piv = jnp.where(offset == d, pltpu.roll(b3, (-d) % 8, 1), piv)
    return piv.reshape(c, dk)


def _gla_kernel(q_ref, k_ref, v_ref, g_ref, r_ref, gg_ref, ltri_ref, mask_ref, o_ref, s_ref, b_ref):
    c = GLA_CHUNK

    @pl.when(pl.program_id(2) == 0)
    def _():
        s_ref[...] = jnp.zeros_like(s_ref)

    n_chunks = q_ref.shape[0] // c
    states = [s_ref[hd] for hd in range(GLA_HPS)]
    for ci in range(n_chunks):
        sl = pl.ds(ci * c, c)
        for hd in range(GLA_HPS):
            ks = slice(hd * GLA_HK, (hd + 1) * GLA_HK)
            vs = slice(hd * GLA_HV, (hd + 1) * GLA_HV)
            slot = ci * GLA_HPS + hd
            q = q_ref[sl, ks]
            k = k_ref[sl, ks]
            g = g_ref[sl, ks]
            v = v_ref[sl, vs]
            g_hi = g.astype(BF16)
            g_lo = (g - g_hi.astype(F32)).astype(BF16)
            x2 = jnp.dot(ltri_ref[...], jnp.concatenate([g_hi, g_lo], axis=1),
                         preferred_element_type=F32)
            b = (x2[:, :GLA_HK] + x2[:, GLA_HK:]) * LOG2E
            b_ref[slot] = b
            eb = jnp.exp2(b)
            ebl = jnp.exp2(b_ref[slot, pl.ds(c - 1, 1), :] - b)

            a = mask_ref[0] * _dot_nt(q.astype(BF16), k.astype(BF16))
            for li, s in enumerate(GLA_LEVELS):
                f = jnp.exp2(-jnp.abs(b - _gla_pivots(b, b_ref.at[slot], s)))
                a = a + mask_ref[li + 1] * _dot_nt((q * f).astype(BF16), (k * f).astype(BF16))

            lhs = jnp.concatenate([(q * eb).astype(BF16), a.astype(BF16)], axis=1)
            rhs = jnp.concatenate([states[hd].astype(BF16), v], axis=0)
            o = jnp.dot(lhs, rhs, preferred_element_type=F32)

            upd = jnp.dot((k * ebl).T.astype(BF16), v, preferred_element_type=F32)
            decay = eb.T[:, c - 1:c]
            states[hd] = decay * states[hd] + upd

            on = _rms(o) * gg_ref[hd]
            o_ref[sl, vs] = (on * r_ref[sl, vs].astype(F32)).astype(BF16)
    for hd in range(GLA_HPS):
        s_ref[hd] = states[hd]


def _gla(qa, ka, va, la, ra, g_gla, ltri, mask, *, batch, seq):
    n = qa.shape[0]
    tiles = seq // GLA_TT
    rows = lambda b, h, t: b * tiles + t
    return pl.pallas_call(
        _gla_kernel,
        out_shape=jax.ShapeDtypeStruct((n, GLA_DV), BF16),
        grid=(batch, GLA_HEADS // GLA_HPS, tiles),
        in_specs=[pl.BlockSpec((GLA_TT, GLA_HPS * GLA_HK), lambda b, h, t: (rows(b, h, t), h)),
                  pl.BlockSpec((GLA_TT, GLA_HPS * GLA_HK), lambda b, h, t: (rows(b, h, t), h)),
                  pl.BlockSpec((GLA_TT, GLA_HPS * GLA_HV), lambda b, h, t: (rows(b, h, t), h)),
                  pl.BlockSpec((GLA_TT, GLA_HPS * GLA_HK), lambda b, h, t: (rows(b, h, t), h)),
                  pl.BlockSpec((GLA_TT, GLA_HPS * GLA_HV), lambda b, h, t: (rows(b, h, t), h)),
                  pl.BlockSpec((GLA_HPS, 1, GLA_HV), lambda b, h, t: (h, 0, 0)),
                  _resident(ltri.shape),
                  _resident(mask.shape)],
        out_specs=pl.BlockSpec((GLA_TT, GLA_HPS * GLA_HV), lambda b, h, t: (rows(b, h, t), h)),
        scratch_shapes=[pltpu.VMEM((GLA_HPS, GLA_HK, GLA_HV), F32),
                        pltpu.VMEM((GLA_HPS * GLA_TT // GLA_CHUNK, GLA_CHUNK, GLA_HK), F32)],
        compiler_params=pltpu.CompilerParams(
            dimension_semantics=("arbitrary", "arbitrary", "arbitrary"),
            vmem_limit_bytes=VMEM_LIMIT),
        name="gla",
    )(qa, ka, va, la, ra, g_gla, ltri, mask)


def _fox_kernel(qe_ref, qo_ref, qen_ref, qon_ref, ke_ref, ko_ref, vt_ref, o_ref,
                m_ref, acc_ref, s0_ref, sa_ref, sb_ref):
    i = pl.program_id(2)
    n_vt = qe_ref.shape[2]
    bq = n_vt * qe_ref.shape[-1]
    bk = bq
    q_cur = (qe_ref, qo_ref)
    q_next = (qen_ref, qon_ref)
    k_refs = (ke_ref, ko_ref)

    m_ref[...] = jnp.full_like(m_ref, -jnp.inf)
    acc_ref[...] = jnp.zeros_like(acc_ref)

    def logits(dst_ref, t, q_refs=q_cur):
        ks = pl.ds(pl.multiple_of(t * bk, bk), bk)
        for hd in range(2):
            qt = jnp.concatenate([q_refs[hd][0, 0, c] for c in range(n_vt)], axis=1)
            dst_ref[hd] = jnp.dot(k_refs[hd][ks, :], qt, preferred_element_type=F32)

    def consume(src_ref, t, masked):
        vt = jnp.concatenate([vt_ref[0, 0, t * n_vt + c] for c in range(n_vt)], axis=1)
        ones = jnp.ones((_FOX_SUM_ROWS, bk), BF16)
        for hd in range(2):
            vaug = jnp.concatenate([vt[hd * FOX_HD:(hd + 1) * FOX_HD], ones], axis=0)
            for c0 in range(0, bq, _FOX_QSUB):
                cs = slice(c0, c0 + _FOX_QSUB)
                s = src_ref[hd, :, cs]
                if masked:
                    k_i = lax.broadcasted_iota(jnp.int32, s.shape, 0)
                    q_i = lax.broadcasted_iota(jnp.int32, s.shape, 1) + c0
                    s = jnp.where(k_i <= q_i, s, -jnp.inf)
                m_old = m_ref[hd, :, cs]
                m_new = jnp.maximum(m_old, jnp.max(s, axis=0, keepdims=True))
                p = jnp.exp2(s - m_new).astype(BF16)
                acc_ref[hd, :, cs] = jnp.exp2(m_old - m_new) * acc_ref[hd, :, cs] + jnp.dot(
                    vaug, p, preferred_element_type=F32)
                m_ref[hd, :, cs] = m_new

    def pair(u, carry):
        logits(sb_ref, 2 * u + 2)
        consume(sa_ref, 2 * u + 1, False)
        logits(sa_ref, 2 * u + 3)
        consume(sb_ref, 2 * u + 2, False)
        return carry

    def head_and_pairs():
        logits(sa_ref, 1)
        consume(s0_ref, 0, False)
        lax.fori_loop(0, (i - 1) // 2, pair, 0)

    @pl.when(i == 0)
    def _():
        logits(s0_ref, 0)
        consume(s0_ref, 0, True)
        logits(s0_ref, 0, q_next)

    @pl.when(i % 2 == 1)
    def _():
        head_and_pairs()
        logits(s0_ref, 0, q_next)
        consume(sa_ref, i, True)

    @pl.when((i > 0) & (i % 2 == 0))
    def _():
        head_and_pairs()
        logits(sb_ref, i)
        consume(sa_ref, i - 1, False)
        logits(s0_ref, 0, q_next)
        consume(sb_ref, i, True)

    oe = acc_ref[0]
    oo = acc_ref[1]
    ot = jnp.concatenate([oe[:FOX_HD] / oe[FOX_HD:FOX_HD + 1],
                          oo[:FOX_HD] / oo[FOX_HD:FOX_HD + 1]], axis=0)
    o_ref[...] = ot.T.astype(BF16)


def _fox(qe, qo, ke, ko, vt, *, batch, seq):
    n = batch * seq
    bq = FOX_BQ
    tiles = seq // bq
    q_blk = (1, 1, bq // vt.shape[-1]) + vt.shape[3:]
    q_spec = pl.BlockSpec(q_blk, lambda b, j, i: (b, j, i, 0, 0))
    q_next_spec = pl.BlockSpec(q_blk, lambda b, j, i: (b, j, jnp.minimum(i + 1, tiles - 1), 0, 0))
    kv_spec = pl.BlockSpec((seq, LANES), lambda b, j, i: (b, j))
    logit_buf = pltpu.VMEM((2, bq, bq), F32)
    return pl.pallas_call(
        _fox_kernel,
        out_shape=jax.ShapeDtypeStruct((n, FOX_W), BF16),
        grid=(batch, FOX_PAIRS, tiles),
        in_specs=[q_spec, q_spec, q_next_spec, q_next_spec, kv_spec, kv_spec,
                  pl.BlockSpec((1, 1) + vt.shape[2:], lambda b, j, i: (b, j, 0, 0, 0))],
        out_specs=pl.BlockSpec((bq, LANES), lambda b, j, i: (b * tiles + i, j)),
        scratch_shapes=[pltpu.VMEM((2, 1, bq), F32),
                        pltpu.VMEM((2, FOX_HD + _FOX_SUM_ROWS, bq), F32),
                        logit_buf, logit_buf, logit_buf],
        compiler_params=pltpu.CompilerParams(
            dimension_semantics=("arbitrary", "arbitrary", "arbitrary"),
            vmem_limit_bytes=VMEM_LIMIT),
        name="fox",
    )(qe, qo, qe, qo, ke, ko, vt)


def _merge_kernel(x_ref, mod_ref, gpost_ref, oa_ref, ob_ref, sga_ref, sgb_ref,
                  wpa_ref, wpb_ref, wout_ref, o_ref):
    pa = jnp.dot(oa_ref[...], wpa_ref[...], preferred_element_type=F32)
    pb = jnp.dot(ob_ref[...], wpb_ref[...], preferred_element_type=F32)
    merged = sga_ref[...].astype(F32) * pa + sgb_ref[...].astype(F32) * pb
    y = jnp.dot(merged.astype(BF16), wout_ref[...], preferred_element_type=F32)
    gt = mod_ref[0, 5:6, :]
    o_ref[...] = x_ref[...] + gt * (_rms(y) * gpost_ref[...])


def _merge(x2d, mods, g_post, oa, ob, sga, sgb, w_pa, w_pb, w_out, *, seq):
    n = x2d.shape[0]
    tm = MERGE_TM
    tiles_per_batch = seq // tm
    row_spec = pl.BlockSpec((tm, D_MODEL), lambda i: (i, 0))
    return pl.pallas_call(
        _merge_kernel,
        out_shape=jax.ShapeDtypeStruct((n, D_MODEL), F32),
        grid=(n // tm,),
        in_specs=[row_spec,
                  pl.BlockSpec((1, N_MOD, D_MODEL), lambda i: (i // tiles_per_batch, 0, 0)),
                  _resident((1, D_MODEL)),
                  row_spec, row_spec, row_spec, row_spec,
                  _resident((D_MODEL, D_MODEL)),
                  _resident((D_MODEL, D_MODEL)),
                  _resident((D_MODEL, D_MODEL))],
        out_specs=row_spec,
        compiler_params=pltpu.CompilerParams(dimension_semantics=("arbitrary",),
                                             vmem_limit_bytes=VMEM_LIMIT),
        name="merge",
    )(x2d, mods, g_post, oa, ob, sga, sgb, w_pa, w_pb, w_out)


def _fox_bias_placement():
    e = np.zeros((LANES, 2 * FOX_W), np.float32)
    for h in range(FOX_HEADS):
        base = LANES * (h // 2) + (FOX_HD if h % 2 == 0 else 0)
        for p in range(3):
            src = _F_LANE0 + FOX_HEADS * p + h
            e[src, base + p] = 1.0
            e[0, base + 3 + p] = 1.0
            e[0, FOX_W + base + p] = 1.0
            e[src, FOX_W + base + 3 + p] = -1.0
    return e


def kernel(x, c, w_ada, b_ada, g_pre, g_post, w_gu1, w_dn1, w_gu2, w_dn2,
           w_in, w_a2, b_a, b_f, g_gla, w_pa, w_pb, w_out):
    batch, seq, d = x.shape
    n = batch * seq
    depth = w_ada.shape[0]
    ltri_np, mask_np = _gla_tables()
    ltri = jnp.asarray(ltri_np, BF16)
    mask = jnp.asarray(mask_np, F32)
    e_qk = jnp.asarray(_fox_bias_placement(), BF16)

    x2d = x.reshape(n, d)
    c_pad = jnp.pad(c, ((0, 16 - batch), (0, 0)))
    for l in range(depth):
        mods = _adaln(c_pad, w_ada[l], b_ada[l][None, :])[:batch].reshape(batch, N_MOD, d)

        x2d = _ffn(x2d, mods, g_pre[l, 0][None, :], g_post[l, 0][None, :],
                   w_gu1[l].astype(BF16), w_dn1[l].astype(BF16), mod0=0, seq=seq)

        w_all = _regroup(jnp.swapaxes(w_in[l], 0, 1))
        w_a2p = jnp.pad(w_a2[l], ((0, LANES - GLA_RANK), (0, 0))).astype(BF16)
        b_small = jnp.pad(b_f[l], (_F_LANE0, LANES - _F_LANE0 - FOX_HEADS))[None, :]

        (qa, ka, va, la, ra, qe, qo, ke, ko, vt, sga, sgb) = _proj(
            x2d, mods, g_pre[l, 1][None, :], w_all, w_a2p, b_a[l][None, :], b_small, e_qk,
            batch=batch, seq=seq)

        oa = _gla(qa, ka, va, la, ra, g_gla[l].reshape(GLA_HEADS, 1, GLA_HV), ltri, mask,
                  batch=batch, seq=seq)
        ob = _fox(qe, qo, ke, ko, vt, batch=batch, seq=seq)

        x2d = _merge(x2d, mods, g_post[l, 1][None, :], oa, ob, sga, sgb,
                     w_pa[l].astype(BF16), w_pb[l].astype(BF16), w_out[l].astype(BF16), seq=seq)

        x2d = _ffn(x2d, mods, g_pre[l, 2][None, :], g_post[l, 2][None, :],
                   w_gu2[l].astype(BF16), w_dn2[l].astype(BF16), mod0=6, seq=seq)
    return x2d.reshape(batch, seq, d)
```

```python
import functools

import numpy as np
import jax
import jax.numpy as jnp
from jax import lax
from jax.experimental import pallas as pl
from jax.experimental.pallas import tpu as pltpu

F32 = jnp.float32
BF16 = jnp.bfloat16

EPS = 1e-6
LOG2E = 1.4426950408889634

D_MODEL = 1024
D_FF = 2816
N_MOD = 9
GLA_HEADS = 4
GLA_DK = 512
GLA_DV = 1024
GLA_HK = GLA_DK // GLA_HEADS
GLA_HV = GLA_DV // GLA_HEADS
GLA_RANK = 16
GLA_TAU = 16.0
FOX_HEADS = 16
FOX_HD = 64
FOX_W = FOX_HEADS * FOX_HD
FOX_PAIRS = FOX_HEADS // 2

LANES = 128
VMEM_LIMIT = 56 * 1024 * 1024

FFN_TM = 1024
FFN_SUB = 512
FFN_CHUNKS = ((0, 1024), (1024, 1024), (2048, 768))
PROJ_TM = 256
GLA_CHUNK = 128
GLA_TT = 1024
GLA_HPS = 1
GLA_LEVELS = (64, 32, 16, 8, 4, 2, 1)
FOX_BQ = 512
_FOX_SUM_ROWS = 16
_FOX_QSUB = 256
MERGE_TM = 512

_C_QA, _C_KA, _C_VA, _C_RA = 0, 512, 1024, 2048
_C_QB, _C_KB, _C_VB, _C_GA, _C_GB = 3072, 4096, 5120, 6144, 7168
_C_SMALL = 8192
_W_COLS = 8320
_F_LANE0 = 16


def _resident(shape):
    nd = len(shape)
    return pl.BlockSpec(shape, lambda *_: (0,) * nd, pipeline_mode=pl.Buffered(1))


def _rms(x):
    return x * lax.rsqrt(jnp.mean(x * x, axis=-1, keepdims=True) + EPS)


def _log_sigmoid(x):
    return jnp.minimum(x, 0.0) - jnp.log(1.0 + jnp.exp(-jnp.abs(x)))


def _adaln_kernel(c_ref, w_ref, b_ref, o_ref):
    c = c_ref[...]
    s = c * jax.nn.sigmoid(c)
    o_ref[...] = jnp.dot(s.astype(BF16), w_ref[...].astype(BF16),
                         preferred_element_type=F32) + b_ref[...]


def _adaln(c_pad, w_ada, b_ada):
    rows = c_pad.shape[0]
    ncol = w_ada.shape[1]
    tn = 1024
    return pl.pallas_call(
        _adaln_kernel,
        out_shape=jax.ShapeDtypeStruct((rows, ncol), F32),
        grid=(ncol // tn,),
        in_specs=[pl.BlockSpec((rows, D_MODEL), lambda j: (0, 0)),
                  pl.BlockSpec((D_MODEL, tn), lambda j: (0, j)),
                  pl.BlockSpec((1, tn), lambda j: (0, j))],
        out_specs=pl.BlockSpec((rows, tn), lambda j: (0, j)),
        compiler_params=pltpu.CompilerParams(dimension_semantics=("arbitrary",)),
        name="adaln",
    )(c_pad, w_ada, b_ada)


def _ffn_kernel(x_ref, mod_ref, gpre_ref, gpost_ref, wgu_ref, wdn_ref, o_ref, *, mod0):
    sh = mod_ref[0, mod0:mod0 + 1, :]
    sc = mod_ref[0, mod0 + 1:mod0 + 2, :]
    gt = mod_ref[0, mod0 + 2:mod0 + 3, :]
    for r0 in range(0, x_ref.shape[0], FFN_SUB):
        rs = pl.ds(r0, FFN_SUB)
        x = x_ref[rs, :]
        h = (_rms(x) * gpre_ref[...]) * (1.0 + sc) + sh
        hb = h.astype(BF16)
        acc = None
        for c0, cw in FFN_CHUNKS:
            g = jnp.dot(hb, wgu_ref[:, c0:c0 + cw], preferred_element_type=F32)
            u = jnp.dot(hb, wgu_ref[:, D_FF + c0:D_FF + c0 + cw], preferred_element_type=F32)
            a = (g * jax.nn.sigmoid(g) * u).astype(BF16)
            part = jnp.dot(a, wdn_ref[c0:c0 + cw, :], preferred_element_type=F32)
            acc = part if acc is None else acc + part
        o_ref[rs, :] = x + (0.5 * gt) * (_rms(acc) * gpost_ref[...])


def _ffn(x2d, mods, g_pre, g_post, w_gu, w_dn, *, mod0, seq):
    n = x2d.shape[0]
    tiles_per_batch = seq // FFN_TM
    return pl.pallas_call(
        functools.partial(_ffn_kernel, mod0=mod0),
        out_shape=jax.ShapeDtypeStruct((n, D_MODEL), F32),
        grid=(n // FFN_TM,),
        in_specs=[pl.BlockSpec((FFN_TM, D_MODEL), lambda i: (i, 0)),
                  pl.BlockSpec((1, N_MOD, D_MODEL), lambda i: (i // tiles_per_batch, 0, 0)),
                  _resident((1, D_MODEL)),
                  _resident((1, D_MODEL)),
                  _resident((D_MODEL, 2 * D_FF)),
                  _resident((D_FF, D_MODEL))],
        out_specs=pl.BlockSpec((FFN_TM, D_MODEL), lambda i: (i, 0)),
        compiler_params=pltpu.CompilerParams(dimension_semantics=("arbitrary",),
                                             vmem_limit_bytes=VMEM_LIMIT),
        name="ffn",
    )(x2d, mods, g_pre, g_post, w_gu, w_dn)


_S_ALOW = 2 * GLA_DK + GLA_DV
_S_WIDE1 = _S_ALOW + GLA_RANK
_S_FB = _S_WIDE1 + GLA_DV + 3 * FOX_W
_S_WIDE2 = _S_FB + FOX_HEADS
_S_END = _S_WIDE2 + 2 * D_MODEL
_REGROUP_TR = 128


def _regroup_kernel(wt_ref, o_ref):
    def blk(src):
        return wt_ref[src:src + LANES, :].T

    def put(dst, src, width):
        for c in range(0, width, LANES):
            o_ref[:, dst + c:dst + c + LANES] = blk(src + c).astype(BF16)

    put(_C_QA, 0, _S_ALOW)
    put(_C_RA, _S_WIDE1, _S_FB - _S_WIDE1)
    put(_C_GA, _S_WIDE2, _S_END - _S_WIDE2)
    a_blk = blk(_S_ALOW)
    f_blk = blk(_S_FB - _F_LANE0)
    lane = lax.broadcasted_iota(jnp.int32, a_blk.shape, 1)
    small = jnp.where(lane < GLA_RANK, a_blk,
                      jnp.where(lane < _F_LANE0 + FOX_HEADS, f_blk, 0.0))
    o_ref[:, _C_SMALL:_W_COLS] = small.astype(BF16)


def _regroup(w_in_t):
    cols, rows = w_in_t.shape
    assert cols == _S_END and (_S_FB - _F_LANE0) % LANES == 0 and _F_LANE0 == GLA_RANK
    return pl.pallas_call(
        _regroup_kernel,
        out_shape=jax.ShapeDtypeStruct((rows, _W_COLS), BF16),
        grid=(rows // _REGROUP_TR,),
        in_specs=[pl.BlockSpec((cols, _REGROUP_TR), lambda i: (0, i))],
        out_specs=pl.BlockSpec((_REGROUP_TR, _W_COLS), lambda i: (i, 0)),
        compiler_params=pltpu.CompilerParams(dimension_semantics=("arbitrary",)),
        name="regroup",
    )(w_in_t)


def _proj_kernel(x_ref, mod_ref, gpre_ref, w_ref, wa2_ref, ba_ref, bsm_ref, eqk_ref,
                 qa_ref, ka_ref, va_ref, la_ref, ra_ref,
                 qe_ref, qo_ref, ke_ref, ko_ref, vt_ref, sga_ref, sgb_ref,
                 fcar_ref, *, tiles_per_batch):
    i = pl.program_id(0)
    tm = x_ref.shape[0]

    @pl.when(i % tiles_per_batch == 0)
    def _():
        fcar_ref[...] = jnp.zeros_like(fcar_ref)

    x = x_ref[...]
    sh = mod_ref[0, 3:4, :]
    sc = mod_ref[0, 4:5, :]
    hb = ((_rms(x) * gpre_ref[...]) * (1.0 + sc) + sh).astype(BF16)

    def proj(c0, cw):
        return jnp.dot(hb, w_ref[:, c0:c0 + cw], preferred_element_type=F32)

    qa_ref[...] = proj(_C_QA, GLA_DK) * (GLA_HK ** -0.5)
    ka_ref[...] = proj(_C_KA, GLA_DK)
    va_ref[...] = proj(_C_VA, GLA_DV).astype(BF16)
    r = proj(_C_RA, GLA_DV)
    ra_ref[...] = (r * jax.nn.sigmoid(r)).astype(BF16)

    zs = proj(_C_SMALL, LANES)
    xa = jnp.dot(zs.astype(BF16), wa2_ref[...], preferred_element_type=F32) + ba_ref[...]
    la_ref[...] = _log_sigmoid(xa) * (1.0 / GLA_TAU)

    lane = lax.broadcasted_iota(jnp.int32, (tm, LANES), 1)
    row = lax.broadcasted_iota(jnp.int32, (tm, LANES), 0)
    in_f = (lane >= _F_LANE0) & (lane < _F_LANE0 + FOX_HEADS)
    f = jnp.where(in_f, _log_sigmoid(zs + bsm_ref[...]), 0.0)
    shift = 1
    while shift < tm:
        f = f + jnp.where(row >= shift, pltpu.roll(f, shift, 0), 0.0)
        shift *= 2
    f = f + fcar_ref[...]
    fcar_ref[...] = f[tm - 1:tm, :]

    f2 = f * LOG2E
    p0 = f2.astype(BF16).astype(F32)
    r1 = f2 - p0
    p1 = r1.astype(BF16).astype(F32)
    p2 = (r1 - p1).astype(BF16).astype(F32)
    fc = p0 + pltpu.roll(p1, FOX_HEADS, 1) + pltpu.roll(p2, 2 * FOX_HEADS, 1)
    fc = jnp.where(lane == 0, 1.0, fc)
    aug = jnp.dot(fc.astype(BF16), eqk_ref[...], preferred_element_type=F32)
    augq = aug[:, :FOX_W]
    augk = aug[:, FOX_W:]

    lane_w = lax.broadcasted_iota(jnp.int32, (tm, FOX_W), 1)
    low_half = (lane_w & (LANES - 1)) < FOX_HD
    zq = proj(_C_QB, FOX_W) * (FOX_HD ** -0.5 * LOG2E)
    q_even = jnp.where(low_half, zq, augq)
    q_odd = jnp.where(low_half, augq, zq)
    for j in range(FOX_PAIRS):
        qe_ref[0, j, 0] = q_even[:, j * LANES:(j + 1) * LANES].T.astype(BF16)
        qo_ref[0, j, 0] = q_odd[:, j * LANES:(j + 1) * LANES].T.astype(BF16)
    zk = proj(_C_KB, FOX_W)
    ke_ref[...] = jnp.where(low_half, zk, augk).astype(BF16)
    ko_ref[...] = jnp.where(low_half, augk, zk).astype(BF16)
    zv = proj(_C_VB, FOX_W)
    for j in range(FOX_PAIRS):
        vt_ref[0, j, 0] = zv[:, j * LANES:(j + 1) * LANES].T.astype(BF16)
    sga_ref[...] = jax.nn.sigmoid(proj(_C_GA, D_MODEL)).astype(BF16)
    sgb_ref[...] = jax.nn.sigmoid(proj(_C_GB, D_MODEL)).astype(BF16)


def _proj(x2d, mods, g_pre, w_all, w_a2p, b_a, b_small, e_qk, *, batch, seq):
    n = x2d.shape[0]
    tm = PROJ_TM
    tiles_per_batch = seq // tm
    row_spec = lambda w: pl.BlockSpec((tm, w), lambda i: (i, 0))
    transposed = jax.ShapeDtypeStruct((batch, FOX_PAIRS, tiles_per_batch, LANES, tm), BF16)
    out_shapes = [
        jax.ShapeDtypeStruct((n, GLA_DK), F32),
        jax.ShapeDtypeStruct((n, GLA_DK), F32),
        jax.ShapeDtypeStruct((n, GLA_DV), BF16),
        jax.ShapeDtypeStruct((n, GLA_DK), F32),
        jax.ShapeDtypeStruct((n, GLA_DV), BF16),
        transposed,
        transposed,
        jax.ShapeDtypeStruct((n, FOX_W), BF16),
        jax.ShapeDtypeStruct((n, FOX_W), BF16),
        transposed,
        jax.ShapeDtypeStruct((n, D_MODEL), BF16),
        jax.ShapeDtypeStruct((n, D_MODEL), BF16),
    ]
    return pl.pallas_call(
        functools.partial(_proj_kernel, tiles_per_batch=tiles_per_batch),
        out_shape=out_shapes,
        grid=(n // tm,),
        in_specs=[row_spec(D_MODEL),
                  pl.BlockSpec((1, N_MOD, D_MODEL), lambda i: (i // tiles_per_batch, 0, 0)),
                  _resident((1, D_MODEL)),
                  _resident((D_MODEL, _W_COLS)),
                  _resident((LANES, GLA_DK)),
                  _resident((1, GLA_DK)),
                  _resident((1, LANES)),
                  _resident((LANES, 2 * FOX_W))],
        out_specs=[row_spec(s.shape[1]) if len(s.shape) == 2 else
                   pl.BlockSpec((1, FOX_PAIRS, 1, LANES, tm),
                                lambda i: (i // tiles_per_batch, 0, i % tiles_per_batch, 0, 0))
                   for s in out_shapes],
        scratch_shapes=[pltpu.VMEM((1, LANES), F32)],
        compiler_params=pltpu.CompilerParams(dimension_semantics=("arbitrary",),
                                             vmem_limit_bytes=VMEM_LIMIT),
        name="proj",
    )(x2d, mods, g_pre, w_all, w_a2p, b_a, b_small, e_qk)


def _gla_tables():
    c = GLA_CHUNK
    t = np.arange(c)[:, None]
    u = np.arange(c)[None, :]
    masks = [(t == u)]
    for s in GLA_LEVELS:
        same = (t // (2 * s)) == (u // (2 * s))
        masks.append(same & (t % (2 * s) >= s) & (u % (2 * s) < s))
    ltri = (u <= t).astype(np.float32)
    mask = np.stack(masks, axis=0).astype(np.float32)
    return ltri, mask


def _dot_nt(a, b):
    return lax.dot_general(a, b, (((1,), (1,)), ((), ())), preferred_element_type=F32)


def _gla_pivots(b, b_rows_ref, s):
    c, dk = b.shape
    if s >= 8:
        return jnp.concatenate(
            [jnp.broadcast_to(b_rows_ref[pl.ds(g0 + s, 1), :], (2 * s, dk))
             for g0 in range(0, c, 2 * s)], axis=0)
    b3 = b.reshape(c // 8, 8, dk)
    sub = lax.broadcasted_iota(jnp.int32, b3.shape, 1)
    offset = s - (sub & (2 * s - 1))
    piv = b3
    for d in range(-(s - 1), s + 1):
        if d != 0:
            piv = jnp.where(offset == d, pltpu.roll(b3, (-d) % 8, 1), piv)
    return piv.reshape(c, dk)


def _gla_kernel(q_ref, k_ref, v_ref, g_ref, r_ref, gg_ref, ltri_ref, mask_ref, o_ref, s_ref, b_ref):
    c = GLA_CHUNK

    @pl.when(pl.program_id(2) == 0)
    def _():
        s_ref[...] = jnp.zeros_like(s_ref)

    n_chunks = q_ref.shape[0] // c
    states = [s_ref[hd] for hd in range(GLA_HPS)]
    for ci in range(n_chunks):
        sl = pl.ds(ci * c, c)
        for hd in range(GLA_HPS):
            ks = slice(hd * GLA_HK, (hd + 1) * GLA_HK)
            vs = slice(hd * GLA_HV, (hd + 1) * GLA_HV)
            slot = ci * GLA_HPS + hd
            q = q_ref[sl, ks]
            k = k_ref[sl, ks]
            g = g_ref[sl, ks]
            v = v_ref[sl, vs]
            g_hi = g.astype(BF16)
            g_lo = (g - g_hi.astype(F32)).astype(BF16)
            x2 = jnp.dot(ltri_ref[...], jnp.concatenate([g_hi, g_lo], axis=1),
                         preferred_element_type=F32)
            b = (x2[:, :GLA_HK] + x2[:, GLA_HK:]) * LOG2E
            b_ref[slot] = b
            eb = jnp.exp2(b)
            ebl = jnp.exp2(b_ref[slot, pl.ds(c - 1, 1), :] - b)

            a = mask_ref[0] * _dot_nt(q.astype(BF16), k.astype(BF16))
            for li, s in enumerate(GLA_LEVELS):
                f = jnp.exp2(-jnp.abs(b - _gla_pivots(b, b_ref.at[slot], s)))
                a = a + mask_ref[li + 1] * _dot_nt((q * f).astype(BF16), (k * f).astype(BF16))

            lhs = jnp.concatenate([(q * eb).astype(BF16), a.astype(BF16)], axis=1)
            rhs = jnp.concatenate([states[hd].astype(BF16), v], axis=0)
            o = jnp.dot(lhs, rhs, preferred_element_type=F32)

            upd = jnp.dot((k * ebl).T.astype(BF16), v, preferred_element_type=F32)
            decay = eb.T[:, c - 1:c]
            states[hd] = decay * states[hd] + upd

            on = _rms(o) * gg_ref[hd]
            o_ref[sl, vs] = (on * r_ref[sl, vs].astype(F32)).astype(BF16)
    for hd in range(GLA_HPS):
        s_ref[hd] = states[hd]


def _gla(qa, ka, va, la, ra, g_gla, ltri, mask, *, batch, seq):
    n = qa.shape[0]
    tiles = seq // GLA_TT
    rows = lambda b, h, t: b * tiles + t
    return pl.pallas_call(
        _gla_kernel,
        out_shape=jax.ShapeDtypeStruct((n, GLA_DV), BF16),
        grid=(batch, GLA_HEADS // GLA_HPS, tiles),
        in_specs=[pl.BlockSpec((GLA_TT, GLA_HPS * GLA_HK), lambda b, h, t: (rows(b, h, t), h)),
                  pl.BlockSpec((GLA_TT, GLA_HPS * GLA_HK), lambda b, h, t: (rows(b, h, t), h)),
                  pl.BlockSpec((GLA_TT, GLA_HPS * GLA_HV), lambda b, h, t: (rows(b, h, t), h)),
                  pl.BlockSpec((GLA_TT, GLA_HPS * GLA_HK), lambda b, h, t: (rows(b, h, t), h)),
                  pl.BlockSpec((GLA_TT, GLA_HPS * GLA_HV), lambda b, h, t: (rows(b, h, t), h)),
                  pl.BlockSpec((GLA_HPS, 1, GLA_HV), lambda b, h, t: (h, 0, 0)),
                  _resident(ltri.shape),
                  _resident(mask.shape)],
        out_specs=pl.BlockSpec((GLA_TT, GLA_HPS * GLA_HV), lambda b, h, t: (rows(b, h, t), h)),
        scratch_shapes=[pltpu.VMEM((GLA_HPS, GLA_HK, GLA_HV), F32),
                        pltpu.VMEM((GLA_HPS * GLA_TT // GLA_CHUNK, GLA_CHUNK, GLA_HK), F32)],
        compiler_params=pltpu.CompilerParams(
            dimension_semantics=("arbitrary", "arbitrary", "arbitrary"),
            vmem_limit_bytes=VMEM_LIMIT),
        name="gla",
    )(qa, ka, va, la, ra, g_gla, ltri, mask)


def _fox_kernel(qe_ref, qo_ref, qen_ref, qon_ref, ke_ref, ko_ref, vt_ref, o_ref,
                m_ref, acc_ref, s0_ref, sa_ref, sb_ref):
    i = pl.program_id(2)
    n_vt = qe_ref.shape[2]
    bq = n_vt * qe_ref.shape[-1]
    bk = bq
    q_cur = (qe_ref, qo_ref)
    q_next = (qen_ref, qon_ref)
    k_refs = (ke_ref, ko_ref)

    m_ref[...] = jnp.full_like(m_ref, -jnp.inf)
    acc_ref[...] = jnp.zeros_like(acc_ref)

    def logits(dst_ref, t, q_refs=q_cur):
        ks = pl.ds(pl.multiple_of(t * bk, bk), bk)
        for hd in range(2):
            qt = jnp.concatenate([q_refs[hd][0, 0, c] for c in range(n_vt)], axis=1)
            dst_ref[hd] = jnp.dot(k_refs[hd][ks, :], qt, preferred_element_type=F32)

    def consume(src_ref, t, masked):
        vt = jnp.concatenate([vt_ref[0, 0, t * n_vt + c] for c in range(n_vt)], axis=1)
        ones = jnp.ones((_FOX_SUM_ROWS, bk), BF16)
        for hd in range(2):
            vaug = jnp.concatenate([vt[hd * FOX_HD:(hd + 1) * FOX_HD], ones], axis=0)
            for c0 in range(0, bq, _FOX_QSUB):
                cs = slice(c0, c0 + _FOX_QSUB)
                s = src_ref[hd, :, cs]
                if masked:
                    k_i = lax.broadcasted_iota(jnp.int32, s.shape, 0)
                    q_i = lax.broadcasted_iota(jnp.int32, s.shape, 1) + c0
                    s = jnp.where(k_i <= q_i, s, -jnp.inf)
                m_old = m_ref[hd, :, cs]
                m_new = jnp.maximum(m_old, jnp.max(s, axis=0, keepdims=True))
                p = jnp.exp2(s - m_new).astype(BF16)
                acc_ref[hd, :, cs] = jnp.exp2(m_old - m_new) * acc_ref[hd, :, cs] + jnp.dot(
                    vaug, p, preferred_element_type=F32)
                m_ref[hd, :, cs] = m_new

    def pair(u, carry=None):
        logits(sb_ref, 2 * u + 2)
        consume(sa_ref, 2 * u + 1, False)
        logits(sa_ref, 2 * u + 3)
        consume(sb_ref, 2 * u + 2, False)
        return carry

    def two_pairs(w, carry):
        pair(2 * w)
        pair(2 * w + 1)
        return carry

    def head_and_pairs():
        logits(sa_ref, 1)
        consume(s0_ref, 0, False)
        n_pairs = (i - 1) // 2
        lax.fori_loop(0, n_pairs // 2, two_pairs, 0)

        @pl.when(n_pairs % 2 == 1)
        def _():
            pair(n_pairs - 1)

    @pl.when(i == 0)
    def _():
        logits(s0_ref, 0)
        consume(s0_ref, 0, True)
        logits(s0_ref, 0, q_next)

    @pl.when(i % 2 == 1)
    def _():
        head_and_pairs()
        logits(s0_ref, 0, q_next)
        consume(sa_ref, i, True)

    @pl.when((i > 0) & (i % 2 == 0))
    def _():
        head_and_pairs()
        logits(sb_ref, i)
        consume(sa_ref, i - 1, False)
        logits(s0_ref, 0, q_next)
        consume(sb_ref, i, True)

    oe = acc_ref[0]
    oo = acc_ref[1]
    ot = jnp.concatenate([oe[:FOX_HD] / oe[FOX_HD:FOX_HD + 1],
                          oo[:FOX_HD] / oo[FOX_HD:FOX_HD + 1]], axis=0)
    o_ref[...] = ot.T.astype(BF16)


def _fox(qe, qo, ke, ko, vt, *, batch, seq):
    n = batch * seq
    bq = FOX_BQ
    tiles = seq // bq
    q_blk = (1, 1, bq // vt.shape[-1]) + vt.shape[3:]
    q_spec = pl.BlockSpec(q_blk, lambda b, j, i: (b, j, i, 0, 0))
    q_next_spec = pl.BlockSpec(q_blk, lambda b, j, i: (b, j, jnp.minimum(i + 1, tiles - 1), 0, 0))
    kv_spec = pl.BlockSpec((seq, LANES), lambda b, j, i: (b, j))
    logit_buf = pltpu.VMEM((2, bq, bq), F32)
    return pl.pallas_call(
        _fox_kernel,
        out_shape=jax.ShapeDtypeStruct((n, FOX_W), BF16),
        grid=(batch, FOX_PAIRS, tiles),
        in_specs=[q_spec, q_spec, q_next_spec, q_next_spec, kv_spec, kv_spec,
                  pl.BlockSpec((1, 1) + vt.shape[2:], lambda b, j, i: (b, j, 0, 0, 0))],
        out_specs=pl.BlockSpec((bq, LANES), lambda b, j, i: (b * tiles + i, j)),
        scratch_shapes=[pltpu.VMEM((2, 1, bq), F32),
                        pltpu.VMEM((2, FOX_HD + _FOX_SUM_ROWS, bq), F32),
                        logit_buf, logit_buf, logit_buf],
        compiler_params=pltpu.CompilerParams(
            dimension_semantics=("arbitrary", "arbitrary", "arbitrary"),
            vmem_limit_bytes=VMEM_LIMIT),
        name="fox",
    )(qe, qo, qe, qo, ke, ko, vt)


def _merge_kernel(x_ref, mod_ref, gpost_ref, oa_ref, ob_ref, sga_ref, sgb_ref,
                  wpa_ref, wpb_ref, wout_ref, o_ref):
    pa = jnp.dot(oa_ref[...], wpa_ref[...], preferred_element_type=F32)
    pb = jnp.dot(ob_ref[...], wpb_ref[...], preferred_element_type=F32)
    merged = sga_ref[...].astype(F32) * pa + sgb_ref[...].astype(F32) * pb
    y = jnp.dot(merged.astype(BF16), wout_ref[...], preferred_element_type=F32)
    gt = mod_ref[0, 5:6, :]
    o_ref[...] = x_ref[...] + gt * (_rms(y) * gpost_ref[...])


def _merge(x2d, mods, g_post, oa, ob, sga, sgb, w_pa, w_pb, w_out, *, seq):
    n = x2d.shape[0]
    tm = MERGE_TM
    tiles_per_batch = seq // tm
    row_spec = pl.BlockSpec((tm, D_MODEL), lambda i: (i, 0))
    return pl.pallas_call(
        _merge_kernel,
        out_shape=jax.ShapeDtypeStruct((n, D_MODEL), F32),
        grid=(n // tm,),
        in_specs=[row_spec,
                  pl.BlockSpec((1, N_MOD, D_MODEL), lambda i: (i // tiles_per_batch, 0, 0)),
                  _resident((1, D_MODEL)),
                  row_spec, row_spec, row_spec, row_spec,
                  _resident((D_MODEL, D_MODEL)),
                  _resident((D_MODEL, D_MODEL)),
                  _resident((D_MODEL, D_MODEL))],
        out_specs=row_spec,
        compiler_params=pltpu.CompilerParams(dimension_semantics=("arbitrary",),
                                             vmem_limit_bytes=VMEM_LIMIT),
        name="merge",
    )(x2d, mods, g_post, oa, ob, sga, sgb, w_pa, w_pb, w_out)


def _fox_bias_placement():
    e = np.zeros((LANES, 2 * FOX_W), np.float32)
    for h in range(FOX_HEADS):
        base = LANES * (h // 2) + (FOX_HD if h % 2 == 0 else 0)
        for p in range(3):
            src = _F_LANE0 + FOX_HEADS * p + h
            e[src, base + p] = 1.0
            e[0, base + 3 + p] = 1.0
            e[0, FOX_W + base + p] = 1.0
            e[src, FOX_W + base + 3 + p] = -1.0
    return e


def kernel(x, c, w_ada, b_ada, g_pre, g_post, w_gu1, w_dn1, w_gu2, w_dn2,
           w_in, w_a2, b_a, b_f, g_gla, w_pa, w_pb, w_out):
    batch, seq, d = x.shape
    n = batch * seq
    depth = w_ada.shape[0]
    ltri_np, mask_np = _gla_tables()
    ltri = jnp.asarray(ltri_np, BF16)
    mask = jnp.asarray(mask_np, F32)
    e_qk = jnp.asarray(_fox_bias_placement(), BF16)

    x2d = x.reshape(n, d)
    c_pad = jnp.pad(c, ((0, 16 - batch), (0, 0)))
    for l in range(depth):
        mods = _adaln(c_pad, w_ada[l], b_ada[l][None, :])[:batch].reshape(batch, N_MOD, d)

        x2d = _ffn(x2d, mods, g_pre[l, 0][None, :], g_post[l, 0][None, :],
                   w_gu1[l].astype(BF16), w_dn1[l].astype(BF16), mod0=0, seq=seq)

        w_all = _regroup(jnp.swapaxes(w_in[l], 0, 1))
        w_a2p = jnp.pad(w_a2[l], ((0, LANES - GLA_RANK), (0, 0))).astype(BF16)
        b_small = jnp.pad(b_f[l], (_F_LANE0, LANES - _F_LANE0 - FOX_HEADS))[None, :]

        (qa, ka, va, la, ra, qe, qo, ke, ko, vt, sga, sgb) = _proj(
            x2d, mods, g_pre[l, 1][None, :], w_all, w_a2p, b_a[l][None, :], b_small, e_qk,
            batch=batch, seq=seq)

        oa = _gla(qa, ka, va, la, ra, g_gla[l].reshape(GLA_HEADS, 1, GLA_HV), ltri, mask,
                  batch=batch, seq=seq)
        ob = _fox(qe, qo, ke, ko, vt, batch=batch, seq=seq)

        x2d = _merge(x2d, mods, g_post[l, 1][None, :], oa, ob, sga, sgb,
                     w_pa[l].astype(BF16), w_pb[l].astype(BF16), w_out[l].astype(BF16), seq=seq)

        x2d = _ffn(x2d, mods, g_pre[l, 2][None, :], g_post[l, 2][None, :],
                   w_gu2[l].astype(BF16), w_dn2[l].astype(BF16), mod0=6, seq=seq)
    return x2d.reshape(batch, seq, d)
```

```python
import functools

import numpy as np
import jax
import jax.numpy as jnp
from jax import lax
from jax.experimental import pallas as pl
from jax.experimental.pallas import tpu as pltpu

F32 = jnp.float32
BF16 = jnp.bfloat16

EPS = 1e-6
LOG2E = 1.4426950408889634

D_MODEL = 1024
D_FF = 2816
N_MOD = 9
GLA_HEADS = 4
GLA_DK = 512
GLA_DV = 1024
GLA_HK = GLA_DK // GLA_HEADS
GLA_HV = GLA_DV // GLA_HEADS
GLA_RANK = 16
GLA_TAU = 16.0
FOX_HEADS = 16
FOX_HD = 64
FOX_W = FOX_HEADS * FOX_HD
FOX_PAIRS = FOX_HEADS // 2

LANES = 128
VMEM_LIMIT = 56 * 1024 * 1024

FFN_TM = 1024
FFN_SUB = 512
FFN_CHUNKS = ((0, 1024), (1024, 1024), (2048, 768))
PROJ_TM = 256
GLA_CHUNK = 128
GLA_TT = 1024
GLA_HPS = 1
GLA_LEVELS = (64, 32, 16, 8, 4, 2, 1)
FOX_BQ = 512
_FOX_SUM_ROWS = 16
_FOX_QSUB = 256
MERGE_TM = 1024
MERGE_SUB = 512

_C_QA, _C_KA, _C_VA, _C_RA = 0, 512, 1024, 2048
_C_QB, _C_KB, _C_VB, _C_GA, _C_GB = 3072, 4096, 5120, 6144, 7168
_C_SMALL = 8192
_W_COLS = 8320
_F_LANE0 = 16


def _resident(shape):
    nd = len(shape)
    return pl.BlockSpec(shape, lambda *_: (0,) * nd, pipeline_mode=pl.Buffered(1))


def _rms(x):
    return x * lax.rsqrt(jnp.mean(x * x, axis=-1, keepdims=True) + EPS)


def _log_sigmoid(x):
    return jnp.minimum(x, 0.0) - jnp.log(1.0 + jnp.exp(-jnp.abs(x)))


def _adaln_kernel(c_ref, w_ref, b_ref, o_ref):
    c = c_ref[...]
    s = c * jax.nn.sigmoid(c)
    o_ref[...] = jnp.dot(s.astype(BF16), w_ref[...].astype(BF16),
                         preferred_element_type=F32) + b_ref[...]


def _adaln(c_pad, w_ada, b_ada):
    rows = c_pad.shape[0]
    ncol = w_ada.shape[1]
    tn = 1024
    return pl.pallas_call(
        _adaln_kernel,
        out_shape=jax.ShapeDtypeStruct((rows, ncol), F32),
        grid=(ncol // tn,),
        in_specs=[pl.BlockSpec((rows, D_MODEL), lambda j: (0, 0)),
                  pl.BlockSpec((D_MODEL, tn), lambda j: (0, j)),
                  pl.BlockSpec((1, tn), lambda j: (0, j))],
        out_specs=pl.BlockSpec((rows, tn), lambda j: (0, j)),
        compiler_params=pltpu.CompilerParams(dimension_semantics=("arbitrary",)),
        name="adaln",
    )(c_pad, w_ada, b_ada)


def _ffn_kernel(x_ref, mod_ref, gpre_ref, gpost_ref, wgu_ref, wdn_ref, o_ref, *, mod0):
    sh = mod_ref[0, mod0:mod0 + 1, :]
    sc = mod_ref[0, mod0 + 1:mod0 + 2, :]
    gt = mod_ref[0, mod0 + 2:mod0 + 3, :]
    for r0 in range(0, x_ref.shape[0], FFN_SUB):
        rs = pl.ds(r0, FFN_SUB)
        x = x_ref[rs, :]
        h = (_rms(x) * gpre_ref[...]) * (1.0 + sc) + sh
        hb = h.astype(BF16)
        acc = None
        for c0, cw in FFN_CHUNKS:
            g = jnp.dot(hb, wgu_ref[:, c0:c0 + cw], preferred_element_type=F32)
            u = jnp.dot(hb, wgu_ref[:, D_FF + c0:D_FF + c0 + cw], preferred_element_type=F32)
            a = (g * jax.nn.sigmoid(g) * u).astype(BF16)
            part = jnp.dot(a, wdn_ref[c0:c0 + cw, :], preferred_element_type=F32)
            acc = part if acc is None else acc + part
        o_ref[rs, :] = x + (0.5 * gt) * (_rms(acc) * gpost_ref[...])


def _ffn(x2d, mods, g_pre, g_post, w_gu, w_dn, *, mod0, seq):
    n = x2d.shape[0]
    tiles_per_batch = seq // FFN_TM
    return pl.pallas_call(
        functools.partial(_ffn_kernel, mod0=mod0),
        out_shape=jax.ShapeDtypeStruct((n, D_MODEL), F32),
        grid=(n // FFN_TM,),
        in_specs=[pl.BlockSpec((FFN_TM, D_MODEL), lambda i: (i, 0)),
                  pl.BlockSpec((1, N_MOD, D_MODEL), lambda i: (i // tiles_per_batch, 0, 0)),
                  _resident((1, D_MODEL)),
                  _resident((1, D_MODEL)),
                  _resident((D_MODEL, 2 * D_FF)),
                  _resident((D_FF, D_MODEL))],
        out_specs=pl.BlockSpec((FFN_TM, D_MODEL), lambda i: (i, 0)),
        compiler_params=pltpu.CompilerParams(dimension_semantics=("arbitrary",),
                                             vmem_limit_bytes=VMEM_LIMIT),
        name="ffn",
    )(x2d, mods, g_pre, g_post, w_gu, w_dn)


_S_ALOW = 2 * GLA_DK + GLA_DV
_S_WIDE1 = _S_ALOW + GLA_RANK
_S_FB = _S_WIDE1 + GLA_DV + 3 * FOX_W
_S_WIDE2 = _S_FB + FOX_HEADS
_S_END = _S_WIDE2 + 2 * D_MODEL
_REGROUP_TR = 128


def _regroup_kernel(wt_ref, o_ref):
    def blk(src):
        return wt_ref[src:src + LANES, :].T

    def put(dst, src, width):
        for c in range(0, width, LANES):
            o_ref[:, dst + c:dst + c + LANES] = blk(src + c).astype(BF16)

    put(_C_QA, 0, _S_ALOW)
    put(_C_RA, _S_WIDE1, _S_FB - _S_WIDE1)
    put(_C_GA, _S_WIDE2, _S_END - _S_WIDE2)
    a_blk = blk(_S_ALOW)
    f_blk = blk(_S_FB - _F_LANE0)
    lane = lax.broadcasted_iota(jnp.int32, a_blk.shape, 1)
    small = jnp.where(lane < GLA_RANK, a_blk,
                      jnp.where(lane < _F_LANE0 + FOX_HEADS, f_blk, 0.0))
    o_ref[:, _C_SMALL:_W_COLS] = small.astype(BF16)


def _regroup(w_in_t):
    cols, rows = w_in_t.shape
    assert cols == _S_END and (_S_FB - _F_LANE0) % LANES == 0 and _F_LANE0 == GLA_RANK
    return pl.pallas_call(
        _regroup_kernel,
        out_shape=jax.ShapeDtypeStruct((rows, _W_COLS), BF16),
        grid=(rows // _REGROUP_TR,),
        in_specs=[pl.BlockSpec((cols, _REGROUP_TR), lambda i: (0, i))],
        out_specs=pl.BlockSpec((_REGROUP_TR, _W_COLS), lambda i: (i, 0)),
        compiler_params=pltpu.CompilerParams(dimension_semantics=("arbitrary",)),
        name="regroup",
    )(w_in_t)


def _proj_kernel(x_ref, mod_ref, gpre_ref, w_ref, wa2_ref, ba_ref, bsm_ref, eqk_ref,
                 qa_ref, ka_ref, va_ref, la_ref, ra_ref,
                 qe_ref, qo_ref, ke_ref, ko_ref, vt_ref, sga_ref, sgb_ref,
                 fcar_ref, *, tiles_per_batch):
    i = pl.program_id(0)
    tm = x_ref.shape[0]

    @pl.when(i % tiles_per_batch == 0)
    def _():
        fcar_ref[...] = jnp.zeros_like(fcar_ref)

    x = x_ref[...]
    sh = mod_ref[0, 3:4, :]
    sc = mod_ref[0, 4:5, :]
    hb = ((_rms(x) * gpre_ref[...]) * (1.0 + sc) + sh).astype(BF16)

    def proj(c0, cw):
        return jnp.dot(hb, w_ref[:, c0:c0 + cw], preferred_element_type=F32)

    qa_ref[...] = proj(_C_QA, GLA_DK) * (GLA_HK ** -0.5)
    ka_ref[...] = proj(_C_KA, GLA_DK)
    va_ref[...] = proj(_C_VA, GLA_DV).astype(BF16)
    r = proj(_C_RA, GLA_DV)
    ra_ref[...] = (r * jax.nn.sigmoid(r)).astype(BF16)

    zs = proj(_C_SMALL, LANES)
    xa = jnp.dot(zs.astype(BF16), wa2_ref[...], preferred_element_type=F32) + ba_ref[...]
    la_ref[...] = _log_sigmoid(xa) * (1.0 / GLA_TAU)

    lane = lax.broadcasted_iota(jnp.int32, (tm, LANES), 1)
    row = lax.broadcasted_iota(jnp.int32, (tm, LANES), 0)
    in_f = (lane >= _F_LANE0) & (lane < _F_LANE0 + FOX_HEADS)
    f = jnp.where(in_f, _log_sigmoid(zs + bsm_ref[...]), 0.0)
    shift = 1
    while shift < tm:
        f = f + jnp.where(row >= shift, pltpu.roll(f, shift, 0), 0.0)
        shift *= 2
    f = f + fcar_ref[...]
    fcar_ref[...] = f[tm - 1:tm, :]

    f2 = f * LOG2E
    p0 = f2.astype(BF16).astype(F32)
    r1 = f2 - p0
    p1 = r1.astype(BF16).astype(F32)
    p2 = (r1 - p1).astype(BF16).astype(F32)
    fc = p0 + pltpu.roll(p1, FOX_HEADS, 1) + pltpu.roll(p2, 2 * FOX_HEADS, 1)
    fc = jnp.where(lane == 0, 1.0, fc)
    aug = jnp.dot(fc.astype(BF16), eqk_ref[...], preferred_element_type=F32)
    augq = aug[:, :FOX_W]
    augk = aug[:, FOX_W:]

    lane_w = lax.broadcasted_iota(jnp.int32, (tm, FOX_W), 1)
    low_half = (lane_w & (LANES - 1)) < FOX_HD
    zq = proj(_C_QB, FOX_W) * (FOX_HD ** -0.5 * LOG2E)
    q_even = jnp.where(low_half, zq, augq)
    q_odd = jnp.where(low_half, augq, zq)
    for j in range(FOX_PAIRS):
        qe_ref[0, j, 0] = q_even[:, j * LANES:(j + 1) * LANES].T.astype(BF16)
        qo_ref[0, j, 0] = q_odd[:, j * LANES:(j + 1) * LANES].T.astype(BF16)
    zk = proj(_C_KB, FOX_W)
    ke_ref[...] = jnp.where(low_half, zk, augk).astype(BF16)
    ko_ref[...] = jnp.where(low_half, augk, zk).astype(BF16)
    zv = proj(_C_VB, FOX_W)
    for j in range(FOX_PAIRS):
        vt_ref[0, j, 0] = zv[:, j * LANES:(j + 1) * LANES].T.astype(BF16)
    sga_ref[...] = jax.nn.sigmoid(proj(_C_GA, D_MODEL)).astype(BF16)
    sgb_ref[...] = jax.nn.sigmoid(proj(_C_GB, D_MODEL)).astype(BF16)


def _proj(x2d, mods, g_pre, w_all, w_a2p, b_a, b_small, e_qk, *, batch, seq):
    n = x2d.shape[0]
    tm = PROJ_TM
    tiles_per_batch = seq // tm
    row_spec = lambda w: pl.BlockSpec((tm, w), lambda i: (i, 0))
    transposed = jax.ShapeDtypeStruct((batch, FOX_PAIRS, tiles_per_batch, LANES, tm), BF16)
    out_shapes = [
        jax.ShapeDtypeStruct((n, GLA_DK), F32),
        jax.ShapeDtypeStruct((n, GLA_DK), F32),
        jax.ShapeDtypeStruct((n, GLA_DV), BF16),
        jax.ShapeDtypeStruct((n, GLA_DK), F32),
        jax.ShapeDtypeStruct((n, GLA_DV), BF16),
        transposed,
        transposed,
        jax.ShapeDtypeStruct((n, FOX_W), BF16),
        jax.ShapeDtypeStruct((n, FOX_W), BF16),
        transposed,
        jax.ShapeDtypeStruct((n, D_MODEL), BF16),
        jax.ShapeDtypeStruct((n, D_MODEL), BF16),
    ]
    return pl.pallas_call(
        functools.partial(_proj_kernel, tiles_per_batch=tiles_per_batch),
        out_shape=out_shapes,
        grid=(n // tm,),
        in_specs=[row_spec(D_MODEL),
                  pl.BlockSpec((1, N_MOD, D_MODEL), lambda i: (i // tiles_per_batch, 0, 0)),
                  _resident((1, D_MODEL)),
                  _resident((D_MODEL, _W_COLS)),
                  _resident((LANES, GLA_DK)),
                  _resident((1, GLA_DK)),
                  _resident((1, LANES)),
                  _resident((LANES, 2 * FOX_W))],
        out_specs=[row_spec(s.shape[1]) if len(s.shape) == 2 else
                   pl.BlockSpec((1, FOX_PAIRS, 1, LANES, tm),
                                lambda i: (i // tiles_per_batch, 0, i % tiles_per_batch, 0, 0))
                   for s in out_shapes],
        scratch_shapes=[pltpu.VMEM((1, LANES), F32)],
        compiler_params=pltpu.CompilerParams(dimension_semantics=("arbitrary",),
                                             vmem_limit_bytes=VMEM_LIMIT),
        name="proj",
    )(x2d, mods, g_pre, w_all, w_a2p, b_a, b_small, e_qk)


def _gla_tables():
    c = GLA_CHUNK
    t = np.arange(c)[:, None]
    u = np.arange(c)[None, :]
    masks = [(t == u)]
    for s in GLA_LEVELS:
        same = (t // (2 * s)) == (u // (2 * s))
        masks.append(same & (t % (2 * s) >= s) & (u % (2 * s) < s))
    ltri = (u <= t).astype(np.float32)
    mask = np.stack(masks, axis=0).astype(np.float32)
    return ltri, mask


def _dot_nt(a, b):
    return lax.dot_general(a, b, (((1,), (1,)), ((), ())), preferred_element_type=F32)


def _gla_pivots(b, b_rows_ref, s):
    c, dk = b.shape
    if s >= 8:
        return jnp.concatenate(
            [jnp.broadcast_to(b_rows_ref[pl.ds(g0 + s, 1), :], (2 * s, dk))
             for g0 in range(0, c, 2 * s)], axis=0)
    b3 = b.reshape(c // 8, 8, dk)
    sub = lax.broadcasted_iota(jnp.int32, b3.shape, 1)
    offset = s - (sub & (2 * s - 1))
    piv = b3
    for d in range(-(s - 1), s + 1):
        if d != 0:
            piv = jnp.where(offset == d, pltpu.roll(b3, (-d) % 8, 1), piv)
    return piv.reshape(c, dk)


def _gla_kernel(q_ref, k_ref, v_ref, g_ref, r_ref, gg_ref, ltri_ref, mask_ref, o_ref, s_ref, b_ref):
    c = GLA_CHUNK

    @pl.when(pl.program_id(2) == 0)
    def _():
        s_ref[...] = jnp.zeros_like(s_ref)

    n_chunks = q_ref.shape[0] // c
    states = [s_ref[hd] for hd in range(GLA_HPS)]
    for ci in range(n_chunks):
        sl = pl.ds(ci * c, c)
        for hd in range(GLA_HPS):
            ks = slice(hd * GLA_HK, (hd + 1) * GLA_HK)
            vs = slice(hd * GLA_HV, (hd + 1) * GLA_HV)
            slot = ci * GLA_HPS + hd
            q = q_ref[sl, ks]
            k = k_ref[sl, ks]
            g = g_ref[sl, ks]
            v = v_ref[sl, vs]
            g_hi = g.astype(BF16)
            g_lo = (g - g_hi.astype(F32)).astype(BF16)
            x2 = jnp.dot(ltri_ref[...], jnp.concatenate([g_hi, g_lo], axis=1),
                         preferred_element_type=F32)
            b = (x2[:, :GLA_HK] + x2[:, GLA_HK:]) * LOG2E
            b_ref[slot] = b
            eb = jnp.exp2(b)
            ebl = jnp.exp2(b_ref[slot, pl.ds(c - 1, 1), :] - b)

            a = mask_ref[0] * _dot_nt(q.astype(BF16), k.astype(BF16))
            for li, s in enumerate(GLA_LEVELS):
                f = jnp.exp2(-jnp.abs(b - _gla_pivots(b, b_ref.at[slot], s)))
                a = a + mask_ref[li + 1] * _dot_nt((q * f).astype(BF16), (k * f).astype(BF16))

            lhs = jnp.concatenate([(q * eb).astype(BF16), a.astype(BF16)], axis=1)
            rhs = jnp.concatenate([states[hd].astype(BF16), v], axis=0)
            o = jnp.dot(lhs, rhs, preferred_element_type=F32)

            upd = jnp.dot((k * ebl).T.astype(BF16), v, preferred_element_type=F32)
            decay = eb.T[:, c - 1:c]
            states[hd] = decay * states[hd] + upd

            on = _rms(o) * gg_ref[hd]
            o_ref[sl, vs] = (on * r_ref[sl, vs].astype(F32)).astype(BF16)
    for hd in range(GLA_HPS):
        s_ref[hd] = states[hd]


def _gla(qa, ka, va, la, ra, g_gla, ltri, mask, *, batch, seq):
    n = qa.shape[0]
    tiles = seq // GLA_TT
    rows = lambda b, h, t: b * tiles + t
    return pl.pallas_call(
        _gla_kernel,
        out_shape=jax.ShapeDtypeStruct((n, GLA_DV), BF16),
        grid=(batch, GLA_HEADS // GLA_HPS, tiles),
        in_specs=[pl.BlockSpec((GLA_TT, GLA_HPS * GLA_HK), lambda b, h, t: (rows(b, h, t), h)),
                  pl.BlockSpec((GLA_TT, GLA_HPS * GLA_HK), lambda b, h, t: (rows(b, h, t), h)),
                  pl.BlockSpec((GLA_TT, GLA_HPS * GLA_HV), lambda b, h, t: (rows(b, h, t), h)),
                  pl.BlockSpec((GLA_TT, GLA_HPS * GLA_HK), lambda b, h, t: (rows(b, h, t), h)),
                  pl.BlockSpec((GLA_TT, GLA_HPS * GLA_HV), lambda b, h, t: (rows(b, h, t), h)),
                  pl.BlockSpec((GLA_HPS, 1, GLA_HV), lambda b, h, t: (h, 0, 0)),
                  _resident(ltri.shape),
                  _resident(mask.shape)],
        out_specs=pl.BlockSpec((GLA_TT, GLA_HPS * GLA_HV), lambda b, h, t: (rows(b, h, t), h)),
        scratch_shapes=[pltpu.VMEM((GLA_HPS, GLA_HK, GLA_HV), F32),
                        pltpu.VMEM((GLA_HPS * GLA_TT // GLA_CHUNK, GLA_CHUNK, GLA_HK), F32)],
        compiler_params=pltpu.CompilerParams(
            dimension_semantics=("arbitrary", "arbitrary", "arbitrary"),
            vmem_limit_bytes=VMEM_LIMIT),
        name="gla",
    )(qa, ka, va, la, ra, g_gla, ltri, mask)


def _fox_kernel(qe_ref, qo_ref, qen_ref, qon_ref, ke_ref, ko_ref, vt_ref, o_ref,
                m_ref, acc_ref, s0_ref, sa_ref, sb_ref):
    i = pl.program_id(2)
    n_vt = qe_ref.shape[2]
    bq = n_vt * qe_ref.shape[-1]
    bk = bq
    q_cur = (qe_ref, qo_ref)
    q_next = (qen_ref, qon_ref)
    k_refs = (ke_ref, ko_ref)

    m_ref[...] = jnp.full_like(m_ref, -jnp.inf)
    acc_ref[...] = jnp.zeros_like(acc_ref)

    def logits(dst_ref, t, q_refs=q_cur):
        ks = pl.ds(pl.multiple_of(t * bk, bk), bk)
        for hd in range(2):
            qt = jnp.concatenate([q_refs[hd][0, 0, c] for c in range(n_vt)], axis=1)
            dst_ref[hd] = jnp.dot(k_refs[hd][ks, :], qt, preferred_element_type=F32)

    def consume(src_ref, t, masked):
        vt = jnp.concatenate([vt_ref[0, 0, t * n_vt + c] for c in range(n_vt)], axis=1)
        ones = jnp.ones((_FOX_SUM_ROWS, bk), BF16)
        for hd in range(2):
            vaug = jnp.concatenate([vt[hd * FOX_HD:(hd + 1) * FOX_HD], ones], axis=0)
            for c0 in range(0, bq, _FOX_QSUB):
                cs = slice(c0, c0 + _FOX_QSUB)
                s = src_ref[hd, :, cs]
                if masked:
                    k_i = lax.broadcasted_iota(jnp.int32, s.shape, 0)
                    q_i = lax.broadcasted_iota(jnp.int32, s.shape, 1) + c0
                    s = jnp.where(k_i <= q_i, s, -jnp.inf)
                m_old = m_ref[hd, :, cs]
                m_new = jnp.maximum(m_old, jnp.max(s, axis=0, keepdims=True))
                p = jnp.exp2(s - m_new).astype(BF16)
                acc_ref[hd, :, cs] = jnp.exp2(m_old - m_new) * acc_ref[hd, :, cs] + jnp.dot(
                    vaug, p, preferred_element_type=F32)
                m_ref[hd, :, cs] = m_new

    def pair(u, carry=None):
        logits(sb_ref, 2 * u + 2)
        consume(sa_ref, 2 * u + 1, False)
        logits(sa_ref, 2 * u + 3)
        consume(sb_ref, 2 * u + 2, False)
        return carry

    def two_pairs(w, carry=None):
        pair(2 * w)
        pair(2 * w + 1)
        return carry

    def four_pairs(z, carry):
        two_pairs(2 * z)
        two_pairs(2 * z + 1)
        return carry

    @pl.when(i == 0)
    def _():
        logits(s0_ref, 0)
        consume(s0_ref, 0, True)
        logits(s0_ref, 0, q_next)

    n_pairs = (i - 1) // 2

    @pl.when(i > 0)
    def _():
        logits(sa_ref, 1)
        consume(s0_ref, 0, False)
        lax.fori_loop(0, n_pairs // 4, four_pairs, 0)

    @pl.when((i > 0) & (n_pairs % 4 >= 2))
    def _():
        two_pairs(2 * (n_pairs // 4))

    @pl.when((i > 0) & (n_pairs % 2 == 1))
    def _():
        pair(n_pairs - 1)

    @pl.when(i % 2 == 1)
    def _():
        logits(s0_ref, 0, q_next)
        consume(sa_ref, i, True)

    @pl.when((i > 0) & (i % 2 == 0))
    def _():
        logits(sb_ref, i)
        consume(sa_ref, i - 1, False)
        logits(s0_ref, 0, q_next)
        consume(sb_ref, i, True)

    oe = acc_ref[0]
    oo = acc_ref[1]
    ot = jnp.concatenate([oe[:FOX_HD] / oe[FOX_HD:FOX_HD + 1],
                          oo[:FOX_HD] / oo[FOX_HD:FOX_HD + 1]], axis=0)
    o_ref[...] = ot.T.astype(BF16)


def _fox(qe, qo, ke, ko, vt, *, batch, seq):
    n = batch * seq
    bq = FOX_BQ
    tiles = seq // bq
    q_blk = (1, 1, bq // vt.shape[-1]) + vt.shape[3:]
    q_spec = pl.BlockSpec(q_blk, lambda b, j, i: (b, j, i, 0, 0))
    q_next_spec = pl.BlockSpec(q_blk, lambda b, j, i: (b, j, jnp.minimum(i + 1, tiles - 1), 0, 0))
    kv_spec = pl.BlockSpec((seq, LANES), lambda b, j, i: (b, j))
    logit_buf = pltpu.VMEM((2, bq, bq), F32)
    return pl.pallas_call(
        _fox_kernel,
        out_shape=jax.ShapeDtypeStruct((n, FOX_W), BF16),
        grid=(batch, FOX_PAIRS, tiles),
        in_specs=[q_spec, q_spec, q_next_spec, q_next_spec, kv_spec, kv_spec,
                  pl.BlockSpec((1, 1) + vt.shape[2:], lambda b, j, i: (b, j, 0, 0, 0))],
        out_specs=pl.BlockSpec((bq, LANES), lambda b, j, i: (b * tiles + i, j)),
        scratch_shapes=[pltpu.VMEM((2, 1, bq), F32),
                        pltpu.VMEM((2, FOX_HD + _FOX_SUM_ROWS, bq), F32),
                        logit_buf, logit_buf, logit_buf],
        compiler_params=pltpu.CompilerParams(
            dimension_semantics=("arbitrary", "arbitrary", "arbitrary"),
            vmem_limit_bytes=VMEM_LIMIT),
        name="fox",
    )(qe, qo, qe, qo, ke, ko, vt)


def _merge_kernel(x_ref, mod_ref, gpost_ref, oa_ref, ob_ref, sga_ref, sgb_ref,
                  wpa_ref, wpb_ref, wout_ref, o_ref):
    gt = mod_ref[0, 5:6, :]
    for r0 in range(0, x_ref.shape[0], MERGE_SUB):
        rs = pl.ds(r0, MERGE_SUB)
        pa = jnp.dot(oa_ref[rs, :], wpa_ref[...], preferred_element_type=F32)
        pb = jnp.dot(ob_ref[rs, :], wpb_ref[...], preferred_element_type=F32)
        merged = sga_ref[rs, :].astype(F32) * pa + sgb_ref[rs, :].astype(F32) * pb
        y = jnp.dot(merged.astype(BF16), wout_ref[...], preferred_element_type=F32)
        o_ref[rs, :] = x_ref[rs, :] + gt * (_rms(y) * gpost_ref[...])


def _merge(x2d, mods, g_post, oa, ob, sga, sgb, w_pa, w_pb, w_out, *, seq):
    n = x2d.shape[0]
    tm = MERGE_TM
    tiles_per_batch = seq // tm
    row_spec = pl.BlockSpec((tm, D_MODEL), lambda i: (i, 0))
    return pl.pallas_call(
        _merge_kernel,
        out_shape=jax.ShapeDtypeStruct((n, D_MODEL), F32),
        grid=(n // tm,),
        in_specs=[row_spec,
                  pl.BlockSpec((1, N_MOD, D_MODEL), lambda i: (i // tiles_per_batch, 0, 0)),
                  _resident((1, D_MODEL)),
                  row_spec, row_spec, row_spec, row_spec,
                  _resident((D_MODEL, D_MODEL)),
                  _resident((D_MODEL, D_MODEL)),
                  _resident((D_MODEL, D_MODEL))],
        out_specs=row_spec,
        compiler_params=pltpu.CompilerParams(dimension_semantics=("arbitrary",),
                                             vmem_limit_bytes=VMEM_LIMIT),
        name="merge",
    )(x2d, mods, g_post, oa, ob, sga, sgb, w_pa, w_pb, w_out)


def _fox_bias_placement():
    e = np.zeros((LANES, 2 * FOX_W), np.float32)
    for h in range(FOX_HEADS):
        base = LANES * (h // 2) + (FOX_HD if h % 2 == 0 else 0)
        for p in range(3):
            src = _F_LANE0 + FOX_HEADS * p + h
            e[src, base + p] = 1.0
            e[0, base + 3 + p] = 1.0
            e[0, FOX_W + base + p] = 1.0
            e[src, FOX_W + base + 3 + p] = -1.0
    return e


def kernel(x, c, w_ada, b_ada, g_pre, g_post, w_gu1, w_dn1, w_gu2, w_dn2,
           w_in, w_a2, b_a, b_f, g_gla, w_pa, w_pb, w_out):
    batch, seq, d = x.shape
    n = batch * seq
    depth = w_ada.shape[0]
    ltri_np, mask_np = _gla_tables()
    ltri = jnp.asarray(ltri_np, BF16)
    mask = jnp.asarray(mask_np, F32)
    e_qk = jnp.asarray(_fox_bias_placement(), BF16)

    x2d = x.reshape(n, d)
    c_pad = jnp.pad(c, ((0, 16 - batch), (0, 0)))
    for l in range(depth):
        mods = _adaln(c_pad, w_ada[l], b_ada[l][None, :])[:batch].reshape(batch, N_MOD, d)

        x2d = _ffn(x2d, mods, g_pre[l, 0][None, :], g_post[l, 0][None, :],
                   w_gu1[l].astype(BF16), w_dn1[l].astype(BF16), mod0=0, seq=seq)

        w_all = _regroup(jnp.swapaxes(w_in[l], 0, 1))
        w_a2p = jnp.pad(w_a2[l], ((0, LANES - GLA_RANK), (0, 0))).astype(BF16)
        b_small = jnp.pad(b_f[l], (_F_LANE0, LANES - _F_LANE0 - FOX_HEADS))[None, :]

        (qa, ka, va, la, ra, qe, qo, ke, ko, vt, sga, sgb) = _proj(
            x2d, mods, g_pre[l, 1][None, :], w_all, w_a2p, b_a[l][None, :], b_small, e_qk,
            batch=batch, seq=seq)

        oa = _gla(qa, ka, va, la, ra, g_gla[l].reshape(GLA_HEADS, 1, GLA_HV), ltri, mask,
                  batch=batch, seq=seq)
        ob = _fox(qe, qo, ke, ko, vt, batch=batch, seq=seq)

        x2d = _merge(x2d, mods, g_post[l, 1][None, :], oa, ob, sga, sgb,
                     w_pa[l].astype(BF16), w_pb[l].astype(BF16), w_out[l].astype(BF16), seq=seq)

        x2d = _ffn(x2d, mods, g_pre[l, 2][None, :], g_post[l, 2][None, :],
                   w_gu2[l].astype(BF16), w_dn2[l].astype(BF16), mod0=6, seq=seq)
    return x2d.reshape(batch, seq, d)
```

```python
import functools

import numpy as np
import jax
import jax.numpy as jnp
from jax import lax
from jax.experimental import pallas as pl
from jax.experimental.pallas import tpu as pltpu

F32 = jnp.float32
BF16 = jnp.bfloat16

EPS = 1e-6
LOG2E = 1.4426950408889634

D_MODEL = 1024
D_FF = 2816
N_MOD = 9
GLA_HEADS = 4
GLA_DK = 512
GLA_DV = 1024
GLA_HK = GLA_DK // GLA_HEADS
GLA_HV = GLA_DV // GLA_HEADS
GLA_RANK = 16
GLA_TAU = 16.0
FOX_HEADS = 16
FOX_HD = 64
FOX_W = FOX_HEADS * FOX_HD
FOX_PAIRS = FOX_HEADS // 2

LANES = 128
VMEM_LIMIT = 56 * 1024 * 1024

FFN_TM = 512
FFN_SUB = 512
FFN_CHUNKS = ((0, 1024), (1024, 1024), (2048, 768))
PROJ_TM = 256
GLA_CHUNK = 128
GLA_TT = 1024
GLA_HPS = 1
GLA_LEVELS = (64, 32, 16, 8, 4, 2, 1)
FOX_BQ = 512
_FOX_SUM_ROWS = 16
_FOX_QSUB = 256
MERGE_TM = 1024
MERGE_SUB = 512

_C_QA, _C_KA, _C_VA, _C_RA = 0, 512, 1024, 2048
_C_QB, _C_KB, _C_VB, _C_GA, _C_GB = 3072, 4096, 5120, 6144, 7168
_C_SMALL = 8192
_W_COLS = 8320
_F_LANE0 = 16


def _resident(shape):
    nd = len(shape)
    return pl.BlockSpec(shape, lambda *_: (0,) * nd, pipeline_mode=pl.Buffered(1))


def _rms(x):
    return x * lax.rsqrt(jnp.mean(x * x, axis=-1, keepdims=True) + EPS)


def _log_sigmoid(x):
    return jnp.minimum(x, 0.0) - jnp.log(1.0 + jnp.exp(-jnp.abs(x)))


def _adaln_kernel(c_ref, w_ref, b_ref, o_ref):
    c = c_ref[...]
    s = c * jax.nn.sigmoid(c)
    o_ref[...] = jnp.dot(s.astype(BF16), w_ref[...].astype(BF16),
                         preferred_element_type=F32) + b_ref[...]


def _adaln(c_pad, w_ada, b_ada):
    rows = c_pad.shape[0]
    ncol = w_ada.shape[1]
    tn = 1024
    return pl.pallas_call(
        _adaln_kernel,
        out_shape=jax.ShapeDtypeStruct((rows, ncol), F32),
        grid=(ncol // tn,),
        in_specs=[pl.BlockSpec((rows, D_MODEL), lambda j: (0, 0)),
                  pl.BlockSpec((D_MODEL, tn), lambda j: (0, j)),
                  pl.BlockSpec((1, tn), lambda j: (0, j))],
        out_specs=pl.BlockSpec((rows, tn), lambda j: (0, j)),
        compiler_params=pltpu.CompilerParams(dimension_semantics=("arbitrary",)),
        name="adaln",
    )(c_pad, w_ada, b_ada)


def _ffn_kernel(x_ref, mod_ref, gpre_ref, gpost_ref, wgu_ref, wdn_ref, o_ref, *, mod0):
    sh = mod_ref[0, mod0:mod0 + 1, :]
    sc = mod_ref[0, mod0 + 1:mod0 + 2, :]
    gt = mod_ref[0, mod0 + 2:mod0 + 3, :]
    for r0 in range(0, x_ref.shape[0], FFN_SUB):
        rs = pl.ds(r0, FFN_SUB)
        x = x_ref[rs, :]
        h = (_rms(x) * gpre_ref[...]) * (1.0 + sc) + sh
        hb = h.astype(BF16)
        acc = None
        for c0, cw in FFN_CHUNKS:
            g = jnp.dot(hb, wgu_ref[:, c0:c0 + cw].astype(BF16), preferred_element_type=F32)
            u = jnp.dot(hb, wgu_ref[:, D_FF + c0:D_FF + c0 + cw].astype(BF16),
                        preferred_element_type=F32)
            a = (g * jax.nn.sigmoid(g) * u).astype(BF16)
            part = jnp.dot(a, wdn_ref[c0:c0 + cw, :].astype(BF16), preferred_element_type=F32)
            acc = part if acc is None else acc + part
        o_ref[rs, :] = x + (0.5 * gt) * (_rms(acc) * gpost_ref[...])


def _ffn(x2d, mods, g_pre, g_post, w_gu, w_dn, *, mod0, seq):
    n = x2d.shape[0]
    tiles_per_batch = seq // FFN_TM
    return pl.pallas_call(
        functools.partial(_ffn_kernel, mod0=mod0),
        out_shape=jax.ShapeDtypeStruct((n, D_MODEL), F32),
        grid=(n // FFN_TM,),
        in_specs=[pl.BlockSpec((FFN_TM, D_MODEL), lambda i: (i, 0)),
                  pl.BlockSpec((1, N_MOD, D_MODEL), lambda i: (i // tiles_per_batch, 0, 0)),
                  _resident((1, D_MODEL)),
                  _resident((1, D_MODEL)),
                  _resident((D_MODEL, 2 * D_FF)),
                  _resident((D_FF, D_MODEL))],
        out_specs=pl.BlockSpec((FFN_TM, D_MODEL), lambda i: (i, 0)),
        compiler_params=pltpu.CompilerParams(dimension_semantics=("arbitrary",),
                                             vmem_limit_bytes=VMEM_LIMIT),
        name="ffn",
    )(x2d, mods, g_pre, g_post, w_gu, w_dn)


_S_ALOW = 2 * GLA_DK + GLA_DV
_S_WIDE1 = _S_ALOW + GLA_RANK
_S_FB = _S_WIDE1 + GLA_DV + 3 * FOX_W
_S_WIDE2 = _S_FB + FOX_HEADS
_S_END = _S_WIDE2 + 2 * D_MODEL
_REGROUP_TR = 128


def _regroup_kernel(wt_ref, o_ref):
    def blk(src):
        return wt_ref[src:src + LANES, :].T

    def put(dst, src, width):
        for c in range(0, width, LANES):
            o_ref[:, dst + c:dst + c + LANES] = blk(src + c).astype(BF16)

    put(_C_QA, 0, _S_ALOW)
    put(_C_RA, _S_WIDE1, _S_FB - _S_WIDE1)
    put(_C_GA, _S_WIDE2, _S_END - _S_WIDE2)
    a_blk = blk(_S_ALOW)
    f_blk = blk(_S_FB - _F_LANE0)
    lane = lax.broadcasted_iota(jnp.int32, a_blk.shape, 1)
    small = jnp.where(lane < GLA_RANK, a_blk,
                      jnp.where(lane < _F_LANE0 + FOX_HEADS, f_blk, 0.0))
    o_ref[:, _C_SMALL:_W_COLS] = small.astype(BF16)


def _regroup(w_in_t):
    cols, rows = w_in_t.shape
    assert cols == _S_END and (_S_FB - _F_LANE0) % LANES == 0 and _F_LANE0 == GLA_RANK
    return pl.pallas_call(
        _regroup_kernel,
        out_shape=jax.ShapeDtypeStruct((rows, _W_COLS), BF16),
        grid=(rows // _REGROUP_TR,),
        in_specs=[pl.BlockSpec((cols, _REGROUP_TR), lambda i: (0, i))],
        out_specs=pl.BlockSpec((_REGROUP_TR, _W_COLS), lambda i: (i, 0)),
        compiler_params=pltpu.CompilerParams(dimension_semantics=("arbitrary",)),
        name="regroup",
    )(w_in_t)


def _proj_kernel(x_ref, mod_ref, gpre_ref, w_ref, wa2_ref, ba_ref, bsm_ref, eqk_ref,
                 qa_ref, ka_ref, va_ref, la_ref, ra_ref,
                 qe_ref, qo_ref, ke_ref, ko_ref, vt_ref, sga_ref, sgb_ref,
                 fcar_ref, *, tiles_per_batch):
    i = pl.program_id(0)
    tm = x_ref.shape[0]

    @pl.when(i % tiles_per_batch == 0)
    def _():
        fcar_ref[...] = jnp.zeros_like(fcar_ref)

    x = x_ref[...]
    sh = mod_ref[0, 3:4, :]
    sc = mod_ref[0, 4:5, :]
    hb = ((_rms(x) * gpre_ref[...]) * (1.0 + sc) + sh).astype(BF16)

    def proj(c0, cw):
        return jnp.dot(hb, w_ref[:, c0:c0 + cw], preferred_element_type=F32)

    qa_ref[...] = proj(_C_QA, GLA_DK) * (GLA_HK ** -0.5)
    ka_ref[...] = proj(_C_KA, GLA_DK)
    va_ref[...] = proj(_C_VA, GLA_DV).astype(BF16)
    r = proj(_C_RA, GLA_DV)
    ra_ref[...] = (r * jax.nn.sigmoid(r)).astype(BF16)

    zs = proj(_C_SMALL, LANES)
    xa = jnp.dot(zs.astype(BF16), wa2_ref[...], preferred_element_type=F32) + ba_ref[...]
    la_ref[...] = _log_sigmoid(xa) * (1.0 / GLA_TAU)

    lane = lax.broadcasted_iota(jnp.int32, (tm, LANES), 1)
    row = lax.broadcasted_iota(jnp.int32, (tm, LANES), 0)
    in_f = (lane >= _F_LANE0) & (lane < _F_LANE0 + FOX_HEADS)
    f = jnp.where(in_f, _log_sigmoid(zs + bsm_ref[...]), 0.0)
    shift = 1
    while shift < tm:
        f = f + jnp.where(row >= shift, pltpu.roll(f, shift, 0), 0.0)
        shift *= 2
    f = f + fcar_ref[...]
    fcar_ref[...] = f[tm - 1:tm, :]

    f2 = f * LOG2E
    p0 = f2.astype(BF16).astype(F32)
    r1 = f2 - p0
    p1 = r1.astype(BF16).astype(F32)
    p2 = (r1 - p1).astype(BF16).astype(F32)
    fc = p0 + pltpu.roll(p1, FOX_HEADS, 1) + pltpu.roll(p2, 2 * FOX_HEADS, 1)
    fc = jnp.where(lane == 0, 1.0, fc)
    aug = jnp.dot(fc.astype(BF16), eqk_ref[...], preferred_element_type=F32)
    augq = aug[:, :FOX_W]
    augk = aug[:, FOX_W:]

    lane_w = lax.broadcasted_iota(jnp.int32, (tm, FOX_W), 1)
    low_half = (lane_w & (LANES - 1)) < FOX_HD
    zq = proj(_C_QB, FOX_W) * (FOX_HD ** -0.5 * LOG2E)
    q_even = jnp.where(low_half, zq, augq)
    q_odd = jnp.where(low_half, augq, zq)
    for j in range(FOX_PAIRS):
        qe_ref[0, j, 0] = q_even[:, j * LANES:(j + 1) * LANES].T.astype(BF16)
        qo_ref[0, j, 0] = q_odd[:, j * LANES:(j + 1) * LANES].T.astype(BF16)
    zk = proj(_C_KB, FOX_W)
    ke_ref[...] = jnp.where(low_half, zk, augk).astype(BF16)
    ko_ref[...] = jnp.where(low_half, augk, zk).astype(BF16)
    zv = proj(_C_VB, FOX_W)
    for j in range(FOX_PAIRS):
        vt_ref[0, j, 0] = zv[:, j * LANES:(j + 1) * LANES].T.astype(BF16)
    sga_ref[...] = jax.nn.sigmoid(proj(_C_GA, D_MODEL)).astype(BF16)
    sgb_ref[...] = jax.nn.sigmoid(proj(_C_GB, D_MODEL)).astype(BF16)


def _proj(x2d, mods, g_pre, w_all, w_a2p, b_a, b_small, e_qk, *, batch, seq):
    n = x2d.shape[0]
    tm = PROJ_TM
    tiles_per_batch = seq // tm
    row_spec = lambda w: pl.BlockSpec((tm, w), lambda i: (i, 0))
    transposed = jax.ShapeDtypeStruct((batch, FOX_PAIRS, tiles_per_batch, LANES, tm), BF16)
    out_shapes = [
        jax.ShapeDtypeStruct((n, GLA_DK), F32),
        jax.ShapeDtypeStruct((n, GLA_DK), F32),
        jax.ShapeDtypeStruct((n, GLA_DV), BF16),
        jax.ShapeDtypeStruct((n, GLA_DK), F32),
        jax.ShapeDtypeStruct((n, GLA_DV), BF16),
        transposed,
        transposed,
        jax.ShapeDtypeStruct((n, FOX_W), BF16),
        jax.ShapeDtypeStruct((n, FOX_W), BF16),
        transposed,
        jax.ShapeDtypeStruct((n, D_MODEL), BF16),
        jax.ShapeDtypeStruct((n, D_MODEL), BF16),
    ]
    return pl.pallas_call(
        functools.partial(_proj_kernel, tiles_per_batch=tiles_per_batch),
        out_shape=out_shapes,
        grid=(n // tm,),
        in_specs=[row_spec(D_MODEL),
                  pl.BlockSpec((1, N_MOD, D_MODEL), lambda i: (i // tiles_per_batch, 0, 0)),
                  _resident((1, D_MODEL)),
                  _resident((D_MODEL, _W_COLS)),
                  _resident((LANES, GLA_DK)),
                  _resident((1, GLA_DK)),
                  _resident((1, LANES)),
                  _resident((LANES, 2 * FOX_W))],
        out_specs=[row_spec(s.shape[1]) if len(s.shape) == 2 else
                   pl.BlockSpec((1, FOX_PAIRS, 1, LANES, tm),
                                lambda i: (i // tiles_per_batch, 0, i % tiles_per_batch, 0, 0))
                   for s in out_shapes],
        scratch_shapes=[pltpu.VMEM((1, LANES), F32)],
        compiler_params=pltpu.CompilerParams(dimension_semantics=("arbitrary",),
                                             vmem_limit_bytes=VMEM_LIMIT),
        name="proj",
    )(x2d, mods, g_pre, w_all, w_a2p, b_a, b_small, e_qk)


def _gla_tables():
    c = GLA_CHUNK
    t = np.arange(c)[:, None]
    u = np.arange(c)[None, :]
    masks = [(t == u)]
    for s in GLA_LEVELS:
        same = (t // (2 * s)) == (u // (2 * s))
        masks.append(same & (t % (2 * s) >= s) & (u % (2 * s) < s))
    ltri = (u <= t).astype(np.float32)
    mask = np.stack(masks, axis=0).astype(np.float32)
    return ltri, mask


def _dot_nt(a, b):
    return lax.dot_general(a, b, (((1,), (1,)), ((), ())), preferred_element_type=F32)


def _gla_pivots(b, b_rows_ref, s):
    c, dk = b.shape
    if s >= 8:
        return jnp.concatenate(
            [jnp.broadcast_to(b_rows_ref[pl.ds(g0 + s, 1), :], (2 * s, dk))
             for g0 in range(0, c, 2 * s)], axis=0)
    b3 = b.reshape(c // 8, 8, dk)
    sub = lax.broadcasted_iota(jnp.int32, b3.shape, 1)
    offset = s - (sub & (2 * s - 1))
    piv = b3
    for d in range(-(s - 1), s + 1):
        if d != 0:
            piv = jnp.where(offset == d, pltpu.roll(b3, (-d) % 8, 1), piv)
    return piv.reshape(c, dk)


def _gla_kernel(q_ref, k_ref, v_ref, g_ref, r_ref, gg_ref, ltri_ref, mask_ref, o_ref, s_ref, b_ref):
    c = GLA_CHUNK

    @pl.when(pl.program_id(2) == 0)
    def _():
        s_ref[...] = jnp.zeros_like(s_ref)

    n_chunks = q_ref.shape[0] // c
    states = [s_ref[hd] for hd in range(GLA_HPS)]
    for ci in range(n_chunks):
        sl = pl.ds(ci * c, c)
        for hd in range(GLA_HPS):
            ks = slice(hd * GLA_HK, (hd + 1) * GLA_HK)
            vs = slice(hd * GLA_HV, (hd + 1) * GLA_HV)
            slot = ci * GLA_HPS + hd
            q = q_ref[sl, ks]
            k = k_ref[sl, ks]
            g = g_ref[sl, ks]
            v = v_ref[sl, vs]
            g_hi = g.astype(BF16)
            g_lo = (g - g_hi.astype(F32)).astype(BF16)
            x2 = jnp.dot(ltri_ref[...], jnp.concatenate([g_hi, g_lo], axis=1),
                         preferred_element_type=F32)
            b = (x2[:, :GLA_HK] + x2[:, GLA_HK:]) * LOG2E
            b_ref[slot] = b
            eb = jnp.exp2(b)
            ebl = jnp.exp2(b_ref[slot, pl.ds(c - 1, 1), :] - b)

            a = mask_ref[0] * _dot_nt(q.astype(BF16), k.astype(BF16))
            for li, s in enumerate(GLA_LEVELS):
                f = jnp.exp2(-jnp.abs(b - _gla_pivots(b, b_ref.at[slot], s)))
                a = a + mask_ref[li + 1] * _dot_nt((q * f).astype(BF16), (k * f).astype(BF16))

            lhs = jnp.concatenate([(q * eb).astype(BF16), a.astype(BF16)], axis=1)
            rhs = jnp.concatenate([states[hd].astype(BF16), v], axis=0)
            o = jnp.dot(lhs, rhs, preferred_element_type=F32)

            upd = jnp.dot((k * ebl).T.astype(BF16), v, preferred_element_type=F32)
            decay = eb.T[:, c - 1:c]
            states[hd] = decay * states[hd] + upd

            on = _rms(o) * gg_ref[hd]
            o_ref[sl, vs] = (on * r_ref[sl, vs].astype(F32)).astype(BF16)
    for hd in range(GLA_HPS):
        s_ref[hd] = states[hd]


def _gla(qa, ka, va, la, ra, g_gla, ltri, mask, *, batch, seq):
    n = qa.shape[0]
    tiles = seq // GLA_TT
    rows = lambda b, h, t: b * tiles + t
    return pl.pallas_call(
        _gla_kernel,
        out_shape=jax.ShapeDtypeStruct((n, GLA_DV), BF16),
        grid=(batch, GLA_HEADS // GLA_HPS, tiles),
        in_specs=[pl.BlockSpec((GLA_TT, GLA_HPS * GLA_HK), lambda b, h, t: (rows(b, h, t), h)),
                  pl.BlockSpec((GLA_TT, GLA_HPS * GLA_HK), lambda b, h, t: (rows(b, h, t), h)),
                  pl.BlockSpec((GLA_TT, GLA_HPS * GLA_HV), lambda b, h, t: (rows(b, h, t), h)),
                  pl.BlockSpec((GLA_TT, GLA_HPS * GLA_HK), lambda b, h, t: (rows(b, h, t), h)),
                  pl.BlockSpec((GLA_TT, GLA_HPS * GLA_HV), lambda b, h, t: (rows(b, h, t), h)),
                  pl.BlockSpec((GLA_HPS, 1, GLA_HV), lambda b, h, t: (h, 0, 0)),
                  _resident(ltri.shape),
                  _resident(mask.shape)],
        out_specs=pl.BlockSpec((GLA_TT, GLA_HPS * GLA_HV), lambda b, h, t: (rows(b, h, t), h)),
        scratch_shapes=[pltpu.VMEM((GLA_HPS, GLA_HK, GLA_HV), F32),
                        pltpu.VMEM((GLA_HPS * GLA_TT // GLA_CHUNK, GLA_CHUNK, GLA_HK), F32)],
        compiler_params=pltpu.CompilerParams(
            dimension_semantics=("arbitrary", "arbitrary", "arbitrary"),
            vmem_limit_bytes=VMEM_LIMIT),
        name="gla",
    )(qa, ka, va, la, ra, g_gla, ltri, mask)


def _fox_kernel(qe_ref, qo_ref, qen_ref, qon_ref, ke_ref, ko_ref, vt_ref, o_ref,
                m_ref, acc_ref, s0_ref, sa_ref, sb_ref):
    i = pl.program_id(2)
    n_vt = qe_ref.shape[2]
    bq = n_vt * qe_ref.shape[-1]
    bk = bq
    q_cur = (qe_ref, qo_ref)
    q_next = (qen_ref, qon_ref)
    k_refs = (ke_ref, ko_ref)

    m_ref[...] = jnp.full_like(m_ref, -jnp.inf)
    acc_ref[...] = jnp.zeros_like(acc_ref)

    def logits(dst_ref, t, q_refs=q_cur):
        ks = pl.ds(pl.multiple_of(t * bk, bk), bk)
        for hd in range(2):
            qt = jnp.concatenate([q_refs[hd][0, 0, c] for c in range(n_vt)], axis=1)
            dst_ref[hd] = jnp.dot(k_refs[hd][ks, :], qt, preferred_element_type=F32)

    def consume(src_ref, t, masked):
        vt = jnp.concatenate([vt_ref[0, 0, t * n_vt + c] for c in range(n_vt)], axis=1)
        ones = jnp.ones((_FOX_SUM_ROWS, bk), BF16)
        for hd in range(2):
            vaug = jnp.concatenate([vt[hd * FOX_HD:(hd + 1) * FOX_HD], ones], axis=0)
            for c0 in range(0, bq, _FOX_QSUB):
                cs = slice(c0, c0 + _FOX_QSUB)
                s = src_ref[hd, :, cs]
                if masked:
                    k_i = lax.broadcasted_iota(jnp.int32, s.shape, 0)
                    q_i = lax.broadcasted_iota(jnp.int32, s.shape, 1) + c0
                    s = jnp.where(k_i <= q_i, s, -jnp.inf)
                m_old = m_ref[hd, :, cs]
                m_new = jnp.maximum(m_old, jnp.max(s, axis=0, keepdims=True))
                p = jnp.exp2(s - m_new).astype(BF16)
                acc_ref[hd, :, cs] = jnp.exp2(m_old - m_new) * acc_ref[hd, :, cs] + jnp.dot(
                    vaug, p, preferred_element_type=F32)
                m_ref[hd, :, cs] = m_new

    def pair(u, carry=None):
        logits(sb_ref, 2 * u + 2)
        consume(sa_ref, 2 * u + 1, False)
        logits(sa_ref, 2 * u + 3)
        consume(sb_ref, 2 * u + 2, False)
        return carry

    def two_pairs(w, carry=None):
        pair(2 * w)
        pair(2 * w + 1)
        return carry

    def four_pairs(z, carry):
        two_pairs(2 * z)
        two_pairs(2 * z + 1)
        return carry

    @pl.when(i == 0)
    def _():
        logits(s0_ref, 0)
        consume(s0_ref, 0, True)
        logits(s0_ref, 0, q_next)

    n_pairs = (i - 1) // 2

    @pl.when(i > 0)
    def _():
        logits(sa_ref, 1)
        consume(s0_ref, 0, False)
        lax.fori_loop(0, n_pairs // 4, four_pairs, 0)

    @pl.when((i > 0) & (n_pairs % 4 >= 2))
    def _():
        two_pairs(2 * (n_pairs // 4))

    @pl.when((i > 0) & (n_pairs % 2 == 1))
    def _():
        pair(n_pairs - 1)

    @pl.when(i % 2 == 1)
    def _():
        logits(s0_ref, 0, q_next)
        consume(sa_ref, i, True)

    @pl.when((i > 0) & (i % 2 == 0))
    def _():
        logits(sb_ref, i)
        consume(sa_ref, i - 1, False)
        logits(s0_ref, 0, q_next)
        consume(sb_ref, i, True)

    oe = acc_ref[0]
    oo = acc_ref[1]
    ot = jnp.concatenate([oe[:FOX_HD] / oe[FOX_HD:FOX_HD + 1],
                          oo[:FOX_HD] / oo[FOX_HD:FOX_HD + 1]], axis=0)
    o_ref[...] = ot.T.astype(BF16)


def _fox(qe, qo, ke, ko, vt, *, batch, seq):
    n = batch * seq
    bq = FOX_BQ
    tiles = seq // bq
    q_blk = (1, 1, bq // vt.shape[-1]) + vt.shape[3:]
    q_spec = pl.BlockSpec(q_blk, lambda b, j, i: (b, j, i, 0, 0))
    q_next_spec = pl.BlockSpec(q_blk, lambda b, j, i: (b, j, jnp.minimum(i + 1, tiles - 1), 0, 0))
    kv_spec = pl.BlockSpec((seq, LANES), lambda b, j, i: (b, j))
    logit_buf = pltpu.VMEM((2, bq, bq), F32)
    return pl.pallas_call(
        _fox_kernel,
        out_shape=jax.ShapeDtypeStruct((n, FOX_W), BF16),
        grid=(batch, FOX_PAIRS, tiles),
        in_specs=[q_spec, q_spec, q_next_spec, q_next_spec, kv_spec, kv_spec,
                  pl.BlockSpec((1, 1) + vt.shape[2:], lambda b, j, i: (b, j, 0, 0, 0))],
        out_specs=pl.BlockSpec((bq, LANES), lambda b, j, i: (b * tiles + i, j)),
        scratch_shapes=[pltpu.VMEM((2, 1, bq), F32),
                        pltpu.VMEM((2, FOX_HD + _FOX_SUM_ROWS, bq), F32),
                        logit_buf, logit_buf, logit_buf],
        compiler_params=pltpu.CompilerParams(
            dimension_semantics=("arbitrary", "arbitrary", "arbitrary"),
            vmem_limit_bytes=VMEM_LIMIT),
        name="fox",
    )(qe, qo, qe, qo, ke, ko, vt)


def _merge_kernel(x_ref, mod_ref, gpost_ref, oa_ref, ob_ref, sga_ref, sgb_ref,
                  wpa_ref, wpb_ref, wout_ref, o_ref):
    gt = mod_ref[0, 5:6, :]
    for r0 in range(0, x_ref.shape[0], MERGE_SUB):
        rs = pl.ds(r0, MERGE_SUB)
        pa = jnp.dot(oa_ref[rs, :], wpa_ref[...].astype(BF16), preferred_element_type=F32)
        pb = jnp.dot(ob_ref[rs, :], wpb_ref[...].astype(BF16), preferred_element_type=F32)
        merged = sga_ref[rs, :].astype(F32) * pa + sgb_ref[rs, :].astype(F32) * pb
        y = jnp.dot(merged.astype(BF16), wout_ref[...].astype(BF16), preferred_element_type=F32)
        o_ref[rs, :] = x_ref[rs, :] + gt * (_rms(y) * gpost_ref[...])


def _merge(x2d, mods, g_post, oa, ob, sga, sgb, w_pa, w_pb, w_out, *, seq):
    n = x2d.shape[0]
    tm = MERGE_TM
    tiles_per_batch = seq // tm
    row_spec = pl.BlockSpec((tm, D_MODEL), lambda i: (i, 0))
    return pl.pallas_call(
        _merge_kernel,
        out_shape=jax.ShapeDtypeStruct((n, D_MODEL), F32),
        grid=(n // tm,),
        in_specs=[row_spec,
                  pl.BlockSpec((1, N_MOD, D_MODEL), lambda i: (i // tiles_per_batch, 0, 0)),
                  _resident((1, D_MODEL)),
                  row_spec, row_spec, row_spec, row_spec,
                  _resident((D_MODEL, D_MODEL)),
                  _resident((D_MODEL, D_MODEL)),
                  _resident((D_MODEL, D_MODEL))],
        out_specs=row_spec,
        compiler_params=pltpu.CompilerParams(dimension_semantics=("arbitrary",),
                                             vmem_limit_bytes=VMEM_LIMIT),
        name="merge",
    )(x2d, mods, g_post, oa, ob, sga, sgb, w_pa, w_pb, w_out)


def _fox_bias_placement():
    e = np.zeros((LANES, 2 * FOX_W), np.float32)
    for h in range(FOX_HEADS):
        base = LANES * (h // 2) + (FOX_HD if h % 2 == 0 else 0)
        for p in range(3):
            src = _F_LANE0 + FOX_HEADS * p + h
            e[src, base + p] = 1.0
            e[0, base + 3 + p] = 1.0
            e[0, FOX_W + base + p] = 1.0
            e[src, FOX_W + base + 3 + p] = -1.0
    return e


def kernel(x, c, w_ada, b_ada, g_pre, g_post, w_gu1, w_dn1, w_gu2, w_dn2,
           w_in, w_a2, b_a, b_f, g_gla, w_pa, w_pb, w_out):
    batch, seq, d = x.shape
    n = batch * seq
    depth = w_ada.shape[0]
    ltri_np, mask_np = _gla_tables()
    ltri = jnp.asarray(ltri_np, BF16)
    mask = jnp.asarray(mask_np, F32)
    e_qk = jnp.asarray(_fox_bias_placement(), BF16)

    x2d = x.reshape(n, d)
    c_pad = jnp.pad(c, ((0, 16 - batch), (0, 0)))
    for l in range(depth):
        mods = _adaln(c_pad, w_ada[l], b_ada[l][None, :])[:batch].reshape(batch, N_MOD, d)

        x2d = _ffn(x2d, mods, g_pre[l, 0][None, :], g_post[l, 0][None, :],
                   w_gu1[l], w_dn1[l], mod0=0, seq=seq)

        w_all = _regroup(jnp.swapaxes(w_in[l], 0, 1))
        w_a2p = jnp.pad(w_a2[l], ((0, LANES - GLA_RANK), (0, 0))).astype(BF16)
        b_small = jnp.pad(b_f[l], (_F_LANE0, LANES - _F_LANE0 - FOX_HEADS))[None, :]

        (qa, ka, va, la, ra, qe, qo, ke, ko, vt, sga, sgb) = _proj(
            x2d, mods, g_pre[l, 1][None, :], w_all, w_a2p, b_a[l][None, :], b_small, e_qk,
            batch=batch, seq=seq)

        oa = _gla(qa, ka, va, la, ra, g_gla[l].reshape(GLA_HEADS, 1, GLA_HV), ltri, mask,
                  batch=batch, seq=seq)
        ob = _fox(qe, qo, ke, ko, vt, batch=batch, seq=seq)

        x2d = _merge(x2d, mods, g_post[l, 1][None, :], oa, ob, sga, sgb,
                     w_pa[l], w_pb[l], w_out[l], seq=seq)

        x2d = _ffn(x2d, mods, g_pre[l, 2][None, :], g_post[l, 2][None, :],
                   w_gu2[l], w_dn2[l], mod0=6, seq=seq)
    return x2d.reshape(batch, seq, d)
```

```python
import functools

import numpy as np
import jax
import jax.numpy as jnp
from jax import lax
from jax.experimental import pallas as pl
from jax.experimental.pallas import tpu as pltpu

F32 = jnp.float32
BF16 = jnp.bfloat16

EPS = 1e-6
LOG2E = 1.4426950408889634

D_MODEL = 1024
D_FF = 2816
N_MOD = 9
GLA_HEADS = 4
GLA_DK = 512
GLA_DV = 1024
GLA_HK = GLA_DK // GLA_HEADS
GLA_HV = GLA_DV // GLA_HEADS
GLA_RANK = 16
GLA_TAU = 16.0
FOX_HEADS = 16
FOX_HD = 64
FOX_W = FOX_HEADS * FOX_HD
FOX_PAIRS = FOX_HEADS // 2

LANES = 128
VMEM_LIMIT = 56 * 1024 * 1024

FFN_TM = 512
FFN_SUB = 512
FFN_CHUNKS = ((0, 1024), (1024, 1024), (2048, 768))
PROJ_TM = 512
PROJ_SUB = 256
GLA_CHUNK = 128
GLA_TT = 1024
GLA_HPS = 1
GLA_LEVELS = (64, 32, 16, 8, 4, 2, 1)
FOX_BQ = 512
_FOX_SUM_ROWS = 16
_FOX_QSUB = 256
MERGE_TM = 1024
MERGE_SUB = 512

_C_QA, _C_KA, _C_VA, _C_RA = 0, 512, 1024, 2048
_C_QB, _C_KB, _C_VB, _C_GA, _C_GB = 3072, 4096, 5120, 6144, 7168
_C_SMALL = 8192
_W_COLS = 8320
_F_LANE0 = 16


def _resident(shape):
    nd = len(shape)
    return pl.BlockSpec(shape, lambda *_: (0,) * nd, pipeline_mode=pl.Buffered(1))


def _rms(x):
    return x * lax.rsqrt(jnp.mean(x * x, axis=-1, keepdims=True) + EPS)


def _log_sigmoid(x):
    return jnp.minimum(x, 0.0) - jnp.log(1.0 + jnp.exp(-jnp.abs(x)))


def _adaln_kernel(c_ref, w_ref, b_ref, o_ref):
    c = c_ref[...]
    s = c * jax.nn.sigmoid(c)
    o_ref[...] = jnp.dot(s.astype(BF16), w_ref[...].astype(BF16),
                         preferred_element_type=F32) + b_ref[...]


def _adaln(c_pad, w_ada, b_ada):
    rows = c_pad.shape[0]
    ncol = w_ada.shape[1]
    tn = 1024
    return pl.pallas_call(
        _adaln_kernel,
        out_shape=jax.ShapeDtypeStruct((rows, ncol), F32),
        grid=(ncol // tn,),
        in_specs=[pl.BlockSpec((rows, D_MODEL), lambda j: (0, 0)),
                  pl.BlockSpec((D_MODEL, tn), lambda j: (0, j)),
                  pl.BlockSpec((1, tn), lambda j: (0, j))],
        out_specs=pl.BlockSpec((rows, tn), lambda j: (0, j)),
        compiler_params=pltpu.CompilerParams(dimension_semantics=("arbitrary",)),
        name="adaln",
    )(c_pad, w_ada, b_ada)


def _ffn_kernel(x_ref, mod_ref, gpre_ref, gpost_ref, wgu_ref, wdn_ref, o_ref, *, mod0):
    sh = mod_ref[0, mod0:mod0 + 1, :]
    sc = mod_ref[0, mod0 + 1:mod0 + 2, :]
    gt = mod_ref[0, mod0 + 2:mod0 + 3, :]
    for r0 in range(0, x_ref.shape[0], FFN_SUB):
        rs = pl.ds(r0, FFN_SUB)
        x = x_ref[rs, :]
        h = (_rms(x) * gpre_ref[...]) * (1.0 + sc) + sh
        hb = h.astype(BF16)
        acc = None
        for c0, cw in FFN_CHUNKS:
            g = jnp.dot(hb, wgu_ref[:, c0:c0 + cw].astype(BF16), preferred_element_type=F32)
            u = jnp.dot(hb, wgu_ref[:, D_FF + c0:D_FF + c0 + cw].astype(BF16),
                        preferred_element_type=F32)
            a = (g * jax.nn.sigmoid(g) * u).astype(BF16)
            part = jnp.dot(a, wdn_ref[c0:c0 + cw, :].astype(BF16), preferred_element_type=F32)
            acc = part if acc is None else acc + part
        o_ref[rs, :] = x + (0.5 * gt) * (_rms(acc) * gpost_ref[...])


def _ffn(x2d, mods, g_pre, g_post, w_gu, w_dn, *, mod0, seq):
    n = x2d.shape[0]
    tiles_per_batch = seq // FFN_TM
    return pl.pallas_call(
        functools.partial(_ffn_kernel, mod0=mod0),
        out_shape=jax.ShapeDtypeStruct((n, D_MODEL), F32),
        grid=(n // FFN_TM,),
        in_specs=[pl.BlockSpec((FFN_TM, D_MODEL), lambda i: (i, 0)),
                  pl.BlockSpec((1, N_MOD, D_MODEL), lambda i: (i // tiles_per_batch, 0, 0)),
                  _resident((1, D_MODEL)),
                  _resident((1, D_MODEL)),
                  _resident((D_MODEL, 2 * D_FF)),
                  _resident((D_FF, D_MODEL))],
        out_specs=pl.BlockSpec((FFN_TM, D_MODEL), lambda i: (i, 0)),
        compiler_params=pltpu.CompilerParams(dimension_semantics=("arbitrary",),
                                             vmem_limit_bytes=VMEM_LIMIT),
        name="ffn",
    )(x2d, mods, g_pre, g_post, w_gu, w_dn)


_S_ALOW = 2 * GLA_DK + GLA_DV
_S_WIDE1 = _S_ALOW + GLA_RANK
_S_FB = _S_WIDE1 + GLA_DV + 3 * FOX_W
_S_WIDE2 = _S_FB + FOX_HEADS
_S_END = _S_WIDE2 + 2 * D_MODEL
_REGROUP_TR = 128


def _regroup_kernel(wt_ref, o_ref):
    def blk(src):
        return wt_ref[src:src + LANES, :].T

    def put(dst, src, width):
        for c in range(0, width, LANES):
            o_ref[:, dst + c:dst + c + LANES] = blk(src + c).astype(BF16)

    put(_C_QA, 0, _S_ALOW)
    put(_C_RA, _S_WIDE1, _S_FB - _S_WIDE1)
    put(_C_GA, _S_WIDE2, _S_END - _S_WIDE2)
    a_blk = blk(_S_ALOW)
    f_blk = blk(_S_FB - _F_LANE0)
    lane = lax.broadcasted_iota(jnp.int32, a_blk.shape, 1)
    small = jnp.where(lane < GLA_RANK, a_blk,
                      jnp.where(lane < _F_LANE0 + FOX_HEADS, f_blk, 0.0))
    o_ref[:, _C_SMALL:_W_COLS] = small.astype(BF16)


def _regroup(w_in_t):
    cols, rows = w_in_t.shape
    assert cols == _S_END and (_S_FB - _F_LANE0) % LANES == 0 and _F_LANE0 == GLA_RANK
    return pl.pallas_call(
        _regroup_kernel,
        out_shape=jax.ShapeDtypeStruct((rows, _W_COLS), BF16),
        grid=(rows // _REGROUP_TR,),
        in_specs=[pl.BlockSpec((cols, _REGROUP_TR), lambda i: (0, i))],
        out_specs=pl.BlockSpec((_REGROUP_TR, _W_COLS), lambda i: (i, 0)),
        compiler_params=pltpu.CompilerParams(dimension_semantics=("arbitrary",)),
        name="regroup",
    )(w_in_t)


def _proj_kernel(x_ref, mod_ref, gpre_ref, w_ref, wa2_ref, ba_ref, bsm_ref, eqk_ref,
                 qa_ref, ka_ref, va_ref, la_ref, ra_ref,
                 qe_ref, qo_ref, ke_ref, ko_ref, vt_ref, sga_ref, sgb_ref,
                 fcar_ref, *, tiles_per_batch):
    i = pl.program_id(0)

    @pl.when(i % tiles_per_batch == 0)
    def _():
        fcar_ref[...] = jnp.zeros_like(fcar_ref)

    for r0 in range(0, x_ref.shape[0], PROJ_SUB):
        _proj_rows(r0, x_ref, mod_ref, gpre_ref, w_ref, wa2_ref, ba_ref, bsm_ref, eqk_ref,
                   qa_ref, ka_ref, va_ref, la_ref, ra_ref,
                   qe_ref, qo_ref, ke_ref, ko_ref, vt_ref, sga_ref, sgb_ref, fcar_ref)


def _proj_rows(r0, x_ref, mod_ref, gpre_ref, w_ref, wa2_ref, ba_ref, bsm_ref, eqk_ref,
               qa_ref, ka_ref, va_ref, la_ref, ra_ref,
               qe_ref, qo_ref, ke_ref, ko_ref, vt_ref, sga_ref, sgb_ref, fcar_ref):
    tm = PROJ_SUB
    rs = pl.ds(r0, tm)
    cs = slice(r0, r0 + tm)
    x = x_ref[rs, :]
    sh = mod_ref[0, 3:4, :]
    sc = mod_ref[0, 4:5, :]
    hb = ((_rms(x) * gpre_ref[...]) * (1.0 + sc) + sh).astype(BF16)

    def proj(c0, cw):
        return jnp.dot(hb, w_ref[:, c0:c0 + cw], preferred_element_type=F32)

    qa_ref[rs, :] = proj(_C_QA, GLA_DK) * (GLA_HK ** -0.5)
    ka_ref[rs, :] = proj(_C_KA, GLA_DK)
    va_ref[rs, :] = proj(_C_VA, GLA_DV).astype(BF16)
    r = proj(_C_RA, GLA_DV)
    ra_ref[rs, :] = (r * jax.nn.sigmoid(r)).astype(BF16)

    zs = proj(_C_SMALL, LANES)
    xa = jnp.dot(zs.astype(BF16), wa2_ref[...], preferred_element_type=F32) + ba_ref[...]
    la_ref[rs, :] = _log_sigmoid(xa) * (1.0 / GLA_TAU)

    lane = lax.broadcasted_iota(jnp.int32, (tm, LANES), 1)
    row = lax.broadcasted_iota(jnp.int32, (tm, LANES), 0)
    in_f = (lane >= _F_LANE0) & (lane < _F_LANE0 + FOX_HEADS)
    f = jnp.where(in_f, _log_sigmoid(zs + bsm_ref[...]), 0.0)
    shift = 1
    while shift < tm:
        f = f + jnp.where(row >= shift, pltpu.roll(f, shift, 0), 0.0)
        shift *= 2
    f = f + fcar_ref[...]
    fcar_ref[...] = f[tm - 1:tm, :]

    f2 = f * LOG2E
    p0 = f2.astype(BF16).astype(F32)
    r1 = f2 - p0
    p1 = r1.astype(BF16).astype(F32)
    p2 = (r1 - p1).astype(BF16).astype(F32)
    fc = p0 + pltpu.roll(p1, FOX_HEADS, 1) + pltpu.roll(p2, 2 * FOX_HEADS, 1)
    fc = jnp.where(lane == 0, 1.0, fc)
    aug = jnp.dot(fc.astype(BF16), eqk_ref[...], preferred_element_type=F32)
    augq = aug[:, :FOX_W]
    augk = aug[:, FOX_W:]

    lane_w = lax.broadcasted_iota(jnp.int32, (tm, FOX_W), 1)
    low_half = (lane_w & (LANES - 1)) < FOX_HD
    zq = proj(_C_QB, FOX_W) * (FOX_HD ** -0.5 * LOG2E)
    q_even = jnp.where(low_half, zq, augq)
    q_odd = jnp.where(low_half, augq, zq)
    for j in range(FOX_PAIRS):
        qe_ref[0, j, 0, :, cs] = q_even[:, j * LANES:(j + 1) * LANES].T.astype(BF16)
        qo_ref[0, j, 0, :, cs] = q_odd[:, j * LANES:(j + 1) * LANES].T.astype(BF16)
    zk = proj(_C_KB, FOX_W)
    ke_ref[rs, :] = jnp.where(low_half, zk, augk).astype(BF16)
    ko_ref[rs, :] = jnp.where(low_half, augk, zk).astype(BF16)
    zv = proj(_C_VB, FOX_W)
    for j in range(FOX_PAIRS):
        vt_ref[0, j, 0, :, cs] = zv[:, j * LANES:(j + 1) * LANES].T.astype(BF16)
    sga_ref[rs, :] = jax.nn.sigmoid(proj(_C_GA, D_MODEL)).astype(BF16)
    sgb_ref[rs, :] = jax.nn.sigmoid(proj(_C_GB, D_MODEL)).astype(BF16)


def _proj(x2d, mods, g_pre, w_all, w_a2p, b_a, b_small, e_qk, *, batch, seq):
    n = x2d.shape[0]
    tm = PROJ_TM
    tiles_per_batch = seq // tm
    row_spec = lambda w: pl.BlockSpec((tm, w), lambda i: (i, 0))
    transposed = jax.ShapeDtypeStruct((batch, FOX_PAIRS, tiles_per_batch, LANES, tm), BF16)
    out_shapes = [
        jax.ShapeDtypeStruct((n, GLA_DK), F32),
        jax.ShapeDtypeStruct((n, GLA_DK), F32),
        jax.ShapeDtypeStruct((n, GLA_DV), BF16),
        jax.ShapeDtypeStruct((n, GLA_DK), F32),
        jax.ShapeDtypeStruct((n, GLA_DV), BF16),
        transposed,
        transposed,
        jax.ShapeDtypeStruct((n, FOX_W), BF16),
        jax.ShapeDtypeStruct((n, FOX_W), BF16),
        transposed,
        jax.ShapeDtypeStruct((n, D_MODEL), BF16),
        jax.ShapeDtypeStruct((n, D_MODEL), BF16),
    ]
    return pl.pallas_call(
        functools.partial(_proj_kernel, tiles_per_batch=tiles_per_batch),
        out_shape=out_shapes,
        grid=(n // tm,),
        in_specs=[row_spec(D_MODEL),
                  pl.BlockSpec((1, N_MOD, D_MODEL), lambda i: (i // tiles_per_batch, 0, 0)),
                  _resident((1, D_MODEL)),
                  _resident((D_MODEL, _W_COLS)),
                  _resident((LANES, GLA_DK)),
                  _resident((1, GLA_DK)),
                  _resident((1, LANES)),
                  _resident((LANES, 2 * FOX_W))],
        out_specs=[row_spec(s.shape[1]) if len(s.shape) == 2 else
                   pl.BlockSpec((1, FOX_PAIRS, 1, LANES, tm),
                                lambda i: (i // tiles_per_batch, 0, i % tiles_per_batch, 0, 0))
                   for s in out_shapes],
        scratch_shapes=[pltpu.VMEM((1, LANES), F32)],
        compiler_params=pltpu.CompilerParams(dimension_semantics=("arbitrary",),
                                             vmem_limit_bytes=VMEM_LIMIT),
        name="proj",
    )(x2d, mods, g_pre, w_all, w_a2p, b_a, b_small, e_qk)


def _gla_tables():
    c = GLA_CHUNK
    t = np.arange(c)[:, None]
    u = np.arange(c)[None, :]
    masks = [(t == u)]
    for s in GLA_LEVELS:
        same = (t // (2 * s)) == (u // (2 * s))
        masks.append(same & (t % (2 * s) >= s) & (u % (2 * s) < s))
    ltri = (u <= t).astype(np.float32)
    mask = np.stack(masks, axis=0).astype(np.float32)
    return ltri, mask


def _dot_nt(a, b):
    return lax.dot_general(a, b, (((1,), (1,)), ((), ())), preferred_element_type=F32)


def _gla_pivots(b, b_rows_ref, s):
    c, dk = b.shape
    if s >= 8:
        return jnp.concatenate(
            [jnp.broadcast_to(b_rows_ref[pl.ds(g0 + s, 1), :], (2 * s, dk))
             for g0 in range(0, c, 2 * s)], axis=0)
    b3 = b.reshape(c // 8, 8, dk)
    sub = lax.broadcasted_iota(jnp.int32, b3.shape, 1)
    offset = s - (sub & (2 * s - 1))
    piv = b3
    for d in range(-(s - 1), s + 1):
        if d != 0:
            piv = jnp.where(offset == d, pltpu.roll(b3, (-d) % 8, 1), piv)
    return piv.reshape(c, dk)


def _gla_kernel(q_ref, k_ref, v_ref, g_ref, r_ref, gg_ref, ltri_ref, mask_ref, o_ref, s_ref, b_ref):
    c = GLA_CHUNK

    @pl.when(pl.program_id(2) == 0)
    def _():
        s_ref[...] = jnp.zeros_like(s_ref)

    n_chunks = q_ref.shape[0] // c
    states = [s_ref[hd] for hd in range(GLA_HPS)]
    for ci in range(n_chunks):
        sl = pl.ds(ci * c, c)
        for hd in range(GLA_HPS):
            ks = slice(hd * GLA_HK, (hd + 1) * GLA_HK)
            vs = slice(hd * GLA_HV, (hd + 1) * GLA_HV)
            slot = ci * GLA_HPS + hd
            q = q_ref[sl, ks]
            k = k_ref[sl, ks]
            g = g_ref[sl, ks]
            v = v_ref[sl, vs]
            g_hi = g.astype(BF16)
            g_lo = (g - g_hi.astype(F32)).astype(BF16)
            x2 = jnp.dot(ltri_ref[...], jnp.concatenate([g_hi, g_lo], axis=1),
                         preferred_element_type=F32)
            b = (x2[:, :GLA_HK] + x2[:, GLA_HK:]) * LOG2E
            b_ref[slot] = b
            eb = jnp.exp2(b)
            ebl = jnp.exp2(b_ref[slot, pl.ds(c - 1, 1), :] - b)

            a = mask_ref[0] * _dot_nt(q.astype(BF16), k.astype(BF16))
            for li, s in enumerate(GLA_LEVELS):
                f = jnp.exp2(-jnp.abs(b - _gla_pivots(b, b_ref.at[slot], s)))
                a = a + mask_ref[li + 1] * _dot_nt((q * f).astype(BF16), (k * f).astype(BF16))

            lhs = jnp.concatenate([(q * eb).astype(BF16), a.astype(BF16)], axis=1)
            rhs = jnp.concatenate([states[hd].astype(BF16), v], axis=0)
            o = jnp.dot(lhs, rhs, preferred_element_type=F32)

            upd = jnp.dot((k * ebl).T.astype(BF16), v, preferred_element_type=F32)
            decay = eb.T[:, c - 1:c]
            states[hd] = decay * states[hd] + upd

            on = _rms(o) * gg_ref[hd]
            o_ref[sl, vs] = (on * r_ref[sl, vs].astype(F32)).astype(BF16)
    for hd in range(GLA_HPS):
        s_ref[hd] = states[hd]


def _gla(qa, ka, va, la, ra, g_gla, ltri, mask, *, batch, seq):
    n = qa.shape[0]
    tiles = seq // GLA_TT
    rows = lambda b, h, t: b * tiles + t
    return pl.pallas_call(
        _gla_kernel,
        out_shape=jax.ShapeDtypeStruct((n, GLA_DV), BF16),
        grid=(batch, GLA_HEADS // GLA_HPS, tiles),
        in_specs=[pl.BlockSpec((GLA_TT, GLA_HPS * GLA_HK), lambda b, h, t: (rows(b, h, t), h)),
                  pl.BlockSpec((GLA_TT, GLA_HPS * GLA_HK), lambda b, h, t: (rows(b, h, t), h)),
                  pl.BlockSpec((GLA_TT, GLA_HPS * GLA_HV), lambda b, h, t: (rows(b, h, t), h)),
                  pl.BlockSpec((GLA_TT, GLA_HPS * GLA_HK), lambda b, h, t: (rows(b, h, t), h)),
                  pl.BlockSpec((GLA_TT, GLA_HPS * GLA_HV), lambda b, h, t: (rows(b, h, t), h)),
                  pl.BlockSpec((GLA_HPS, 1, GLA_HV), lambda b, h, t: (h, 0, 0)),
                  _resident(ltri.shape),
                  _resident(mask.shape)],
        out_specs=pl.BlockSpec((GLA_TT, GLA_HPS * GLA_HV), lambda b, h, t: (rows(b, h, t), h)),
        scratch_shapes=[pltpu.VMEM((GLA_HPS, GLA_HK, GLA_HV), F32),
                        pltpu.VMEM((GLA_HPS * GLA_TT // GLA_CHUNK, GLA_CHUNK, GLA_HK), F32)],
        compiler_params=pltpu.CompilerParams(
            dimension_semantics=("arbitrary", "arbitrary", "arbitrary"),
            vmem_limit_bytes=VMEM_LIMIT),
        name="gla",
    )(qa, ka, va, la, ra, g_gla, ltri, mask)


def _fox_kernel(qe_ref, qo_ref, qen_ref, qon_ref, ke_ref, ko_ref, vt_ref, o_ref,
                m_ref, acc_ref, s0_ref, sa_ref, sb_ref):
    i = pl.program_id(2)
    n_vt = qe_ref.shape[2]
    bq = n_vt * qe_ref.shape[-1]
    bk = bq
    q_cur = (qe_ref, qo_ref)
    q_next = (qen_ref, qon_ref)
    k_refs = (ke_ref, ko_ref)

    m_ref[...] = jnp.full_like(m_ref, -jnp.inf)
    acc_ref[...] = jnp.zeros_like(acc_ref)

    def logits(dst_ref, t, q_refs=q_cur):
        ks = pl.ds(pl.multiple_of(t * bk, bk), bk)
        for hd in range(2):
            qt = jnp.concatenate([q_refs[hd][0, 0, c] for c in range(n_vt)], axis=1)
            dst_ref[hd] = jnp.dot(k_refs[hd][ks, :], qt, preferred_element_type=F32)

    def consume(src_ref, t, masked):
        vt = jnp.concatenate([vt_ref[0, 0, t * n_vt + c] for c in range(n_vt)], axis=1)
        ones = jnp.ones((_FOX_SUM_ROWS, bk), BF16)
        for hd in range(2):
            vaug = jnp.concatenate([vt[hd * FOX_HD:(hd + 1) * FOX_HD], ones], axis=0)
            for c0 in range(0, bq, _FOX_QSUB):
                cs = slice(c0, c0 + _FOX_QSUB)
                s = src_ref[hd, :, cs]
                if masked:
                    k_i = lax.broadcasted_iota(jnp.int32, s.shape, 0)
                    q_i = lax.broadcasted_iota(jnp.int32, s.shape, 1) + c0
                    s = jnp.where(k_i <= q_i, s, -jnp.inf)
                m_old = m_ref[hd, :, cs]
                m_new = jnp.maximum(m_old, jnp.max(s, axis=0, keepdims=True))
                p = jnp.exp2(s - m_new).astype(BF16)
                acc_ref[hd, :, cs] = jnp.exp2(m_old - m_new) * acc_ref[hd, :, cs] + jnp.dot(
                    vaug, p, preferred_element_type=F32)
                m_ref[hd, :, cs] = m_new

    def pair(u, carry=None):
        logits(sb_ref, 2 * u + 2)
        consume(sa_ref, 2 * u + 1, False)
        logits(sa_ref, 2 * u + 3)
        consume(sb_ref, 2 * u + 2, False)
        return carry

    def two_pairs(w, carry=None):
        pair(2 * w)
        pair(2 * w + 1)
        return carry

    def four_pairs(z, carry):
        two_pairs(2 * z)
        two_pairs(2 * z + 1)
        return carry

    @pl.when(i == 0)
    def _():
        logits(s0_ref, 0)
        consume(s0_ref, 0, True)
        logits(s0_ref, 0, q_next)

    n_pairs = (i - 1) // 2

    @pl.when(i > 0)
    def _():
        logits(sa_ref, 1)
        consume(s0_ref, 0, False)
        lax.fori_loop(0, n_pairs // 4, four_pairs, 0)

    @pl.when((i > 0) & (n_pairs % 4 >= 2))
    def _():
        two_pairs(2 * (n_pairs // 4))

    @pl.when((i > 0) & (n_pairs % 2 == 1))
    def _():
        pair(n_pairs - 1)

    @pl.when(i % 2 == 1)
    def _():
        logits(s0_ref, 0, q_next)
        consume(sa_ref, i, True)

    @pl.when((i > 0) & (i % 2 == 0))
    def _():
        logits(sb_ref, i)
        consume(sa_ref, i - 1, False)
        logits(s0_ref, 0, q_next)
        consume(sb_ref, i, True)

    oe = acc_ref[0]
    oo = acc_ref[1]
    ot = jnp.concatenate([oe[:FOX_HD] / oe[FOX_HD:FOX_HD + 1],
                          oo[:FOX_HD] / oo[FOX_HD:FOX_HD + 1]], axis=0)
    o_ref[...] = ot.T.astype(BF16)


def _fox(qe, qo, ke, ko, vt, *, batch, seq):
    n = batch * seq
    bq = FOX_BQ
    tiles = seq // bq
    q_blk = (1, 1, bq // vt.shape[-1]) + vt.shape[3:]
    q_spec = pl.BlockSpec(q_blk, lambda b, j, i: (b, j, i, 0, 0))
    q_next_spec = pl.BlockSpec(q_blk, lambda b, j, i: (b, j, jnp.minimum(i + 1, tiles - 1), 0, 0))
    kv_spec = pl.BlockSpec((seq, LANES), lambda b, j, i: (b, j))
    logit_buf = pltpu.VMEM((2, bq, bq), F32)
    return pl.pallas_call(
        _fox_kernel,
        out_shape=jax.ShapeDtypeStruct((n, FOX_W), BF16),
        grid=(batch, FOX_PAIRS, tiles),
        in_specs=[q_spec, q_spec, q_next_spec, q_next_spec, kv_spec, kv_spec,
                  pl.BlockSpec((1, 1) + vt.shape[2:], lambda b, j, i: (b, j, 0, 0, 0))],
        out_specs=pl.BlockSpec((bq, LANES), lambda b, j, i: (b * tiles + i, j)),
        scratch_shapes=[pltpu.VMEM((2, 1, bq), F32),
                        pltpu.VMEM((2, FOX_HD + _FOX_SUM_ROWS, bq), F32),
                        logit_buf, logit_buf, logit_buf],
        compiler_params=pltpu.CompilerParams(
            dimension_semantics=("arbitrary", "arbitrary", "arbitrary"),
            vmem_limit_bytes=VMEM_LIMIT),
        name="fox",
    )(qe, qo, qe, qo, ke, ko, vt)


def _merge_kernel(x_ref, mod_ref, gpost_ref, oa_ref, ob_ref, sga_ref, sgb_ref,
                  wpa_ref, wpb_ref, wout_ref, o_ref):
    gt = mod_ref[0, 5:6, :]
    for r0 in range(0, x_ref.shape[0], MERGE_SUB):
        rs = pl.ds(r0, MERGE_SUB)
        pa = jnp.dot(oa_ref[rs, :], wpa_ref[...].astype(BF16), preferred_element_type=F32)
        pb = jnp.dot(ob_ref[rs, :], wpb_ref[...].astype(BF16), preferred_element_type=F32)
        merged = sga_ref[rs, :].astype(F32) * pa + sgb_ref[rs, :].astype(F32) * pb
        y = jnp.dot(merged.astype(BF16), wout_ref[...].astype(BF16), preferred_element_type=F32)
        o_ref[rs, :] = x_ref[rs, :] + gt * (_rms(y) * gpost_ref[...])


def _merge(x2d, mods, g_post, oa, ob, sga, sgb, w_pa, w_pb, w_out, *, seq):
    n = x2d.shape[0]
    tm = MERGE_TM
    tiles_per_batch = seq // tm
    row_spec = pl.BlockSpec((tm, D_MODEL), lambda i: (i, 0))
    return pl.pallas_call(
        _merge_kernel,
        out_shape=jax.ShapeDtypeStruct((n, D_MODEL), F32),
        grid=(n // tm,),
        in_specs=[row_spec,
                  pl.BlockSpec((1, N_MOD, D_MODEL), lambda i: (i // tiles_per_batch, 0, 0)),
                  _resident((1, D_MODEL)),
                  row_spec, row_spec, row_spec, row_spec,
                  _resident((D_MODEL, D_MODEL)),
                  _resident((D_MODEL, D_MODEL)),
                  _resident((D_MODEL, D_MODEL))],
        out_specs=row_spec,
        compiler_params=pltpu.CompilerParams(dimension_semantics=("arbitrary",),
                                             vmem_limit_bytes=VMEM_LIMIT),
        name="merge",
    )(x2d, mods, g_post, oa, ob, sga, sgb, w_pa, w_pb, w_out)


def _fox_bias_placement():
    e = np.zeros((LANES, 2 * FOX_W), np.float32)
    for h in range(FOX_HEADS):
        base = LANES * (h // 2) + (FOX_HD if h % 2 == 0 else 0)
        for p in range(3):
            src = _F_LANE0 + FOX_HEADS * p + h
            e[src, base + p] = 1.0
            e[0, base + 3 + p] = 1.0
            e[0, FOX_W + base + p] = 1.0
            e[src, FOX_W + base + 3 + p] = -1.0
    return e


def kernel(x, c, w_ada, b_ada, g_pre, g_post, w_gu1, w_dn1, w_gu2, w_dn2,
           w_in, w_a2, b_a, b_f, g_gla, w_pa, w_pb, w_out):
    batch, seq, d = x.shape
    n = batch * seq
    depth = w_ada.shape[0]
    ltri_np, mask_np = _gla_tables()
    ltri = jnp.asarray(ltri_np, BF16)
    mask = jnp.asarray(mask_np, F32)
    e_qk = jnp.asarray(_fox_bias_placement(), BF16)

    x2d = x.reshape(n, d)
    c_pad = jnp.pad(c, ((0, 16 - batch), (0, 0)))
    for l in range(depth):
        mods = _adaln(c_pad, w_ada[l], b_ada[l][None, :])[:batch].reshape(batch, N_MOD, d)

        x2d = _ffn(x2d, mods, g_pre[l, 0][None, :], g_post[l, 0][None, :],
                   w_gu1[l], w_dn1[l], mod0=0, seq=seq)

        w_all = _regroup(jnp.swapaxes(w_in[l], 0, 1))
        w_a2p = jnp.pad(w_a2[l], ((0, LANES - GLA_RANK), (0, 0))).astype(BF16)
        b_small = jnp.pad(b_f[l], (_F_LANE0, LANES - _F_LANE0 - FOX_HEADS))[None, :]

        (qa, ka, va, la, ra, qe, qo, ke, ko, vt, sga, sgb) = _proj(
            x2d, mods, g_pre[l, 1][None, :], w_all, w_a2p, b_a[l][None, :], b_small, e_qk,
            batch=batch, seq=seq)

        oa = _gla(qa, ka, va, la, ra, g_gla[l].reshape(GLA_HEADS, 1, GLA_HV), ltri, mask,
                  batch=batch, seq=seq)
        ob = _fox(qe, qo, ke, ko, vt, batch=batch, seq=seq)

        x2d = _merge(x2d, mods, g_post[l, 1][None, :], oa, ob, sga, sgb,
                     w_pa[l], w_pb[l], w_out[l], seq=seq)

        x2d = _ffn(x2d, mods, g_pre[l, 2][None, :], g_post[l, 2][None, :],
                   w_gu2[l], w_dn2[l], mod0=6, seq=seq)
    return x2d.reshape(batch, seq, d)
```

```python
import functools

import numpy as np
import jax
import jax.numpy as jnp
from jax import lax
from jax.experimental import pallas as pl
from jax.experimental.pallas import tpu as pltpu

F32 = jnp.float32
BF16 = jnp.bfloat16

EPS = 1e-6
LOG2E = 1.4426950408889634

D_MODEL = 1024
D_FF = 2816
N_MOD = 9
GLA_HEADS = 4
GLA_DK = 512
GLA_DV = 1024
GLA_HK = GLA_DK // GLA_HEADS
GLA_HV = GLA_DV // GLA_HEADS
GLA_RANK = 16
GLA_TAU = 16.0
FOX_HEADS = 16
FOX_HD = 64
FOX_W = FOX_HEADS * FOX_HD
FOX_PAIRS = FOX_HEADS // 2

LANES = 128
VMEM_LIMIT = 56 * 1024 * 1024

FFN_TM = 512
FFN_SUB = 256
FFN_CHUNKS = ((0, 1024), (1024, 1024), (2048, 768))
PROJ_TM = 512
PROJ_SUB = 256
GLA_CHUNK = 128
GLA_TT = 1024
GLA_HPS = 1
GLA_LEVELS = (64, 32, 16, 8, 4, 2, 1)
FOX_BQ = 512
_FOX_SUM_ROWS = 16
_FOX_QSUB = 256
MERGE_TM = 1024
MERGE_SUB = 512

_C_QA, _C_KA, _C_VA, _C_RA = 0, 512, 1024, 2048
_C_QB, _C_KB, _C_VB, _C_GA, _C_GB = 3072, 4096, 5120, 6144, 7168
_C_SMALL = 8192
_W_COLS = 8320
_F_LANE0 = 16


def _resident(shape):
    nd = len(shape)
    return pl.BlockSpec(shape, lambda *_: (0,) * nd, pipeline_mode=pl.Buffered(1))


def _rms(x):
    return x * lax.rsqrt(jnp.mean(x * x, axis=-1, keepdims=True) + EPS)


def _log_sigmoid(x):
    return jnp.minimum(x, 0.0) - jnp.log(1.0 + jnp.exp(-jnp.abs(x)))


def _adaln_kernel(c_ref, w_ref, b_ref, o_ref):
    c = c_ref[...]
    s = c * jax.nn.sigmoid(c)
    o_ref[...] = jnp.dot(s.astype(BF16), w_ref[...].astype(BF16),
                         preferred_element_type=F32) + b_ref[...]


def _adaln(c_pad, w_ada, b_ada):
    rows = c_pad.shape[0]
    ncol = w_ada.shape[1]
    tn = 1024
    return pl.pallas_call(
        _adaln_kernel,
        out_shape=jax.ShapeDtypeStruct((rows, ncol), F32),
        grid=(ncol // tn,),
        in_specs=[pl.BlockSpec((rows, D_MODEL), lambda j: (0, 0)),
                  pl.BlockSpec((D_MODEL, tn), lambda j: (0, j)),
                  pl.BlockSpec((1, tn), lambda j: (0, j))],
        out_specs=pl.BlockSpec((rows, tn), lambda j: (0, j)),
        compiler_params=pltpu.CompilerParams(dimension_semantics=("arbitrary",)),
        name="adaln",
    )(c_pad, w_ada, b_ada)


def _ffn_kernel(x_ref, mod_ref, gpre_ref, gpost_ref, wgu_ref, wdn_ref, o_ref, *, mod0):
    sh = mod_ref[0, mod0:mod0 + 1, :]
    sc = mod_ref[0, mod0 + 1:mod0 + 2, :]
    gt = mod_ref[0, mod0 + 2:mod0 + 3, :]
    for r0 in range(0, x_ref.shape[0], FFN_SUB):
        rs = pl.ds(r0, FFN_SUB)
        x = x_ref[rs, :]
        h = (_rms(x) * gpre_ref[...]) * (1.0 + sc) + sh
        hb = h.astype(BF16)
        acc = None
        for c0, cw in FFN_CHUNKS:
            g = jnp.dot(hb, wgu_ref[:, c0:c0 + cw].astype(BF16), preferred_element_type=F32)
            u = jnp.dot(hb, wgu_ref[:, D_FF + c0:D_FF + c0 + cw].astype(BF16),
                        preferred_element_type=F32)
            a = (g * jax.nn.sigmoid(g) * u).astype(BF16)
            part = jnp.dot(a, wdn_ref[c0:c0 + cw, :].astype(BF16), preferred_element_type=F32)
            acc = part if acc is None else acc + part
        o_ref[rs, :] = x + (0.5 * gt) * (_rms(acc) * gpost_ref[...])


def _ffn(x2d, mods, g_pre, g_post, w_gu, w_dn, *, mod0, seq):
    n = x2d.shape[0]
    tiles_per_batch = seq // FFN_TM
    return pl.pallas_call(
        functools.partial(_ffn_kernel, mod0=mod0),
        out_shape=jax.ShapeDtypeStruct((n, D_MODEL), F32),
        grid=(n // FFN_TM,),
        in_specs=[pl.BlockSpec((FFN_TM, D_MODEL), lambda i: (i, 0)),
                  pl.BlockSpec((1, N_MOD, D_MODEL), lambda i: (i // tiles_per_batch, 0, 0)),
                  _resident((1, D_MODEL)),
                  _resident((1, D_MODEL)),
                  _resident((D_MODEL, 2 * D_FF)),
                  _resident((D_FF, D_MODEL))],
        out_specs=pl.BlockSpec((FFN_TM, D_MODEL), lambda i: (i, 0)),
        compiler_params=pltpu.CompilerParams(dimension_semantics=("arbitrary",),
                                             vmem_limit_bytes=VMEM_LIMIT),
        name="ffn",
    )(x2d, mods, g_pre, g_post, w_gu, w_dn)


_S_ALOW = 2 * GLA_DK + GLA_DV
_S_WIDE1 = _S_ALOW + GLA_RANK
_S_FB = _S_WIDE1 + GLA_DV + 3 * FOX_W
_S_WIDE2 = _S_FB + FOX_HEADS
_S_END = _S_WIDE2 + 2 * D_MODEL
_REGROUP_TR = 128


def _regroup_kernel(wt_ref, o_ref):
    def blk(src):
        return wt_ref[src:src + LANES, :].T

    def put(dst, src, width):
        for c in range(0, width, LANES):
            o_ref[:, dst + c:dst + c + LANES] = blk(src + c).astype(BF16)

    put(_C_QA, 0, _S_ALOW)
    put(_C_RA, _S_WIDE1, _S_FB - _S_WIDE1)
    put(_C_GA, _S_WIDE2, _S_END - _S_WIDE2)
    a_blk = blk(_S_ALOW)
    f_blk = blk(_S_FB - _F_LANE0)
    lane = lax.broadcasted_iota(jnp.int32, a_blk.shape, 1)
    small = jnp.where(lane < GLA_RANK, a_blk,
                      jnp.where(lane < _F_LANE0 + FOX_HEADS, f_blk, 0.0))
    o_ref[:, _C_SMALL:_W_COLS] = small.astype(BF16)


def _regroup(w_in_t):
    cols, rows = w_in_t.shape
    assert cols == _S_END and (_S_FB - _F_LANE0) % LANES == 0 and _F_LANE0 == GLA_RANK
    return pl.pallas_call(
        _regroup_kernel,
        out_shape=jax.ShapeDtypeStruct((rows, _W_COLS), BF16),
        grid=(rows // _REGROUP_TR,),
        in_specs=[pl.BlockSpec((cols, _REGROUP_TR), lambda i: (0, i))],
        out_specs=pl.BlockSpec((_REGROUP_TR, _W_COLS), lambda i: (i, 0)),
        compiler_params=pltpu.CompilerParams(dimension_semantics=("arbitrary",)),
        name="regroup",
    )(w_in_t)


def _proj_kernel(x_ref, mod_ref, gpre_ref, w_ref, wa2_ref, ba_ref, bsm_ref, eqk_ref,
                 qa_ref, ka_ref, va_ref, la_ref, ra_ref,
                 qe_ref, qo_ref, ke_ref, ko_ref, vt_ref, sga_ref, sgb_ref,
                 fcar_ref, *, tiles_per_batch):
    i = pl.program_id(0)

    @pl.when(i % tiles_per_batch == 0)
    def _():
        fcar_ref[...] = jnp.zeros_like(fcar_ref)

    for r0 in range(0, x_ref.shape[0], PROJ_SUB):
        _proj_rows(r0, x_ref, mod_ref, gpre_ref, w_ref, wa2_ref, ba_ref, bsm_ref, eqk_ref,
                   qa_ref, ka_ref, va_ref, la_ref, ra_ref,
                   qe_ref, qo_ref, ke_ref, ko_ref, vt_ref, sga_ref, sgb_ref, fcar_ref)


def _proj_rows(r0, x_ref, mod_ref, gpre_ref, w_ref, wa2_ref, ba_ref, bsm_ref, eqk_ref,
               qa_ref, ka_ref, va_ref, la_ref, ra_ref,
               qe_ref, qo_ref, ke_ref, ko_ref, vt_ref, sga_ref, sgb_ref, fcar_ref):
    tm = PROJ_SUB
    rs = pl.ds(r0, tm)
    cs = slice(r0, r0 + tm)
    x = x_ref[rs, :]
    sh = mod_ref[0, 3:4, :]
    sc = mod_ref[0, 4:5, :]
    hb = ((_rms(x) * gpre_ref[...]) * (1.0 + sc) + sh).astype(BF16)

    def proj(c0, cw):
        return jnp.dot(hb, w_ref[:, c0:c0 + cw], preferred_element_type=F32)

    qa_ref[rs, :] = proj(_C_QA, GLA_DK) * (GLA_HK ** -0.5)
    ka_ref[rs, :] = proj(_C_KA, GLA_DK)
    va_ref[rs, :] = proj(_C_VA, GLA_DV).astype(BF16)
    r = proj(_C_RA, GLA_DV)
    ra_ref[rs, :] = (r * jax.nn.sigmoid(r)).astype(BF16)

    zs = proj(_C_SMALL, LANES)
    xa = jnp.dot(zs.astype(BF16), wa2_ref[...], preferred_element_type=F32) + ba_ref[...]
    la_ref[rs, :] = _log_sigmoid(xa) * (1.0 / GLA_TAU)

    lane = lax.broadcasted_iota(jnp.int32, (tm, LANES), 1)
    row = lax.broadcasted_iota(jnp.int32, (tm, LANES), 0)
    in_f = (lane >= _F_LANE0) & (lane < _F_LANE0 + FOX_HEADS)
    f = jnp.where(in_f, _log_sigmoid(zs + bsm_ref[...]), 0.0)
    shift = 1
    while shift < tm:
        f = f + jnp.where(row >= shift, pltpu.roll(f, shift, 0), 0.0)
        shift *= 2
    f = f + fcar_ref[...]
    fcar_ref[...] = f[tm - 1:tm, :]

    f2 = f * LOG2E
    p0 = f2.astype(BF16).astype(F32)
    r1 = f2 - p0
    p1 = r1.astype(BF16).astype(F32)
    p2 = (r1 - p1).astype(BF16).astype(F32)
    fc = p0 + pltpu.roll(p1, FOX_HEADS, 1) + pltpu.roll(p2, 2 * FOX_HEADS, 1)
    fc = jnp.where(lane == 0, 1.0, fc)
    aug = jnp.dot(fc.astype(BF16), eqk_ref[...], preferred_element_type=F32)
    augq = aug[:, :FOX_W]
    augk = aug[:, FOX_W:]

    lane_w = lax.broadcasted_iota(jnp.int32, (tm, FOX_W), 1)
    low_half = (lane_w & (LANES - 1)) < FOX_HD
    zq = proj(_C_QB, FOX_W) * (FOX_HD ** -0.5 * LOG2E)
    q_even = jnp.where(low_half, zq, augq)
    q_odd = jnp.where(low_half, augq, zq)
    for j in range(FOX_PAIRS):
        qe_ref[0, j, 0, :, cs] = q_even[:, j * LANES:(j + 1) * LANES].T.astype(BF16)
        qo_ref[0, j, 0, :, cs] = q_odd[:, j * LANES:(j + 1) * LANES].T.astype(BF16)
    zk = proj(_C_KB, FOX_W)
    ke_ref[rs, :] = jnp.where(low_half, zk, augk).astype(BF16)
    ko_ref[rs, :] = jnp.where(low_half, augk, zk).astype(BF16)
    zv = proj(_C_VB, FOX_W)
    for j in range(FOX_PAIRS):
        vt_ref[0, j, 0, :, cs] = zv[:, j * LANES:(j + 1) * LANES].T.astype(BF16)
    sga_ref[rs, :] = jax.nn.sigmoid(proj(_C_GA, D_MODEL)).astype(BF16)
    sgb_ref[rs, :] = jax.nn.sigmoid(proj(_C_GB, D_MODEL)).astype(BF16)


def _proj(x2d, mods, g_pre, w_all, w_a2p, b_a, b_small, e_qk, *, batch, seq):
    n = x2d.shape[0]
    tm = PROJ_TM
    tiles_per_batch = seq // tm
    row_spec = lambda w: pl.BlockSpec((tm, w), lambda i: (i, 0))
    transposed = jax.ShapeDtypeStruct((batch, FOX_PAIRS, tiles_per_batch, LANES, tm), BF16)
    out_shapes = [
        jax.ShapeDtypeStruct((n, GLA_DK), F32),
        jax.ShapeDtypeStruct((n, GLA_DK), F32),
        jax.ShapeDtypeStruct((n, GLA_DV), BF16),
        jax.ShapeDtypeStruct((n, GLA_DK), F32),
        jax.ShapeDtypeStruct((n, GLA_DV), BF16),
        transposed,
        transposed,
        jax.ShapeDtypeStruct((n, FOX_W), BF16),
        jax.ShapeDtypeStruct((n, FOX_W), BF16),
        transposed,
        jax.ShapeDtypeStruct((n, D_MODEL), BF16),
        jax.ShapeDtypeStruct((n, D_MODEL), BF16),
    ]
    return pl.pallas_call(
        functools.partial(_proj_kernel, tiles_per_batch=tiles_per_batch),
        out_shape=out_shapes,
        grid=(n // tm,),
        in_specs=[row_spec(D_MODEL),
                  pl.BlockSpec((1, N_MOD, D_MODEL), lambda i: (i // tiles_per_batch, 0, 0)),
                  _resident((1, D_MODEL)),
                  _resident((D_MODEL, _W_COLS)),
                  _resident((LANES, GLA_DK)),
                  _resident((1, GLA_DK)),
                  _resident((1, LANES)),
                  _resident((LANES, 2 * FOX_W))],
        out_specs=[row_spec(s.shape[1]) if len(s.shape) == 2 else
                   pl.BlockSpec((1, FOX_PAIRS, 1, LANES, tm),
                                lambda i: (i // tiles_per_batch, 0, i % tiles_per_batch, 0, 0))
                   for s in out_shapes],
        scratch_shapes=[pltpu.VMEM((1, LANES), F32)],
        compiler_params=pltpu.CompilerParams(dimension_semantics=("arbitrary",),
                                             vmem_limit_bytes=VMEM_LIMIT),
        name="proj",
    )(x2d, mods, g_pre, w_all, w_a2p, b_a, b_small, e_qk)


def _gla_tables():
    c = GLA_CHUNK
    t = np.arange(c)[:, None]
    u = np.arange(c)[None, :]
    masks = [(t == u)]
    for s in GLA_LEVELS:
        same = (t // (2 * s)) == (u // (2 * s))
        masks.append(same & (t % (2 * s) >= s) & (u % (2 * s) < s))
    ltri = (u <= t).astype(np.float32)
    mask = np.stack(masks, axis=0).astype(np.float32)
    return ltri, mask


def _dot_nt(a, b):
    return lax.dot_general(a, b, (((1,), (1,)), ((), ())), preferred_element_type=F32)


def _gla_pivots(b, b_rows_ref, s):
    c, dk = b.shape
    if s >= 8:
        return jnp.concatenate(
            [jnp.broadcast_to(b_rows_ref[pl.ds(g0 + s, 1), :], (2 * s, dk))
             for g0 in range(0, c, 2 * s)], axis=0)
    b3 = b.reshape(c // 8, 8, dk)
    sub = lax.broadcasted_iota(jnp.int32, b3.shape, 1)
    offset = s - (sub & (2 * s - 1))
    piv = b3
    for d in range(-(s - 1), s + 1):
        if d != 0:
            piv = jnp.where(offset == d, pltpu.roll(b3, (-d) % 8, 1), piv)
    return piv.reshape(c, dk)


def _gla_kernel(q_ref, k_ref, v_ref, g_ref, r_ref, gg_ref, ltri_ref, mask_ref, o_ref, s_ref, b_ref):
    c = GLA_CHUNK

    @pl.when(pl.program_id(2) == 0)
    def _():
        s_ref[...] = jnp.zeros_like(s_ref)

    n_chunks = q_ref.shape[0] // c
    states = [s_ref[hd] for hd in range(GLA_HPS)]
    for ci in range(n_chunks):
        sl = pl.ds(ci * c, c)
        for hd in range(GLA_HPS):
            ks = slice(hd * GLA_HK, (hd + 1) * GLA_HK)
            vs = slice(hd * GLA_HV, (hd + 1) * GLA_HV)
            slot = ci * GLA_HPS + hd
            q = q_ref[sl, ks]
            k = k_ref[sl, ks]
            g = g_ref[sl, ks]
            v = v_ref[sl, vs]
            g_hi = g.astype(BF16)
            g_lo = (g - g_hi.astype(F32)).astype(BF16)
            x2 = jnp.dot(ltri_ref[...], jnp.concatenate([g_hi, g_lo], axis=1),
                         preferred_element_type=F32)
            b = (x2[:, :GLA_HK] + x2[:, GLA_HK:]) * LOG2E
            b_ref[slot] = b
            eb = jnp.exp2(b)
            ebl = jnp.exp2(b_ref[slot, pl.ds(c - 1, 1), :] - b)

            a = mask_ref[0] * _dot_nt(q.astype(BF16), k.astype(BF16))
            for li, s in enumerate(GLA_LEVELS):
                f = jnp.exp2(-jnp.abs(b - _gla_pivots(b, b_ref.at[slot], s)))
                a = a + mask_ref[li + 1] * _dot_nt((q * f).astype(BF16), (k * f).astype(BF16))

            lhs = jnp.concatenate([(q * eb).astype(BF16), a.astype(BF16)], axis=1)
            rhs = jnp.concatenate([states[hd].astype(BF16), v], axis=0)
            o = jnp.dot(lhs, rhs, preferred_element_type=F32)

            upd = jnp.dot((k * ebl).T.astype(BF16), v, preferred_element_type=F32)
            decay = eb.T[:, c - 1:c]
            states[hd] = decay * states[hd] + upd

            on = _rms(o) * gg_ref[hd]
            o_ref[sl, vs] = (on * r_ref[sl, vs].astype(F32)).astype(BF16)
    for hd in range(GLA_HPS):
        s_ref[hd] = states[hd]


def _gla(qa, ka, va, la, ra, g_gla, ltri, mask, *, batch, seq):
    n = qa.shape[0]
    tiles = seq // GLA_TT
    rows = lambda b, h, t: b * tiles + t
    return pl.pallas_call(
        _gla_kernel,
        out_shape=jax.ShapeDtypeStruct((n, GLA_DV), BF16),
        grid=(batch, GLA_HEADS // GLA_HPS, tiles),
        in_specs=[pl.BlockSpec((GLA_TT, GLA_HPS * GLA_HK), lambda b, h, t: (rows(b, h, t), h)),
                  pl.BlockSpec((GLA_TT, GLA_HPS * GLA_HK), lambda b, h, t: (rows(b, h, t), h)),
                  pl.BlockSpec((GLA_TT, GLA_HPS * GLA_HV), lambda b, h, t: (rows(b, h, t), h)),
                  pl.BlockSpec((GLA_TT, GLA_HPS * GLA_HK), lambda b, h, t: (rows(b, h, t), h)),
                  pl.BlockSpec((GLA_TT, GLA_HPS * GLA_HV), lambda b, h, t: (rows(b, h, t), h)),
                  pl.BlockSpec((GLA_HPS, 1, GLA_HV), lambda b, h, t: (h, 0, 0)),
                  _resident(ltri.shape),
                  _resident(mask.shape)],
        out_specs=pl.BlockSpec((GLA_TT, GLA_HPS * GLA_HV), lambda b, h, t: (rows(b, h, t), h)),
        scratch_shapes=[pltpu.VMEM((GLA_HPS, GLA_HK, GLA_HV), F32),
                        pltpu.VMEM((GLA_HPS * GLA_TT // GLA_CHUNK, GLA_CHUNK, GLA_HK), F32)],
        compiler_params=pltpu.CompilerParams(
            dimension_semantics=("arbitrary", "arbitrary", "arbitrary"),
            vmem_limit_bytes=VMEM_LIMIT),
        name="gla",
    )(qa, ka, va, la, ra, g_gla, ltri, mask)


def _fox_kernel(qe_ref, qo_ref, qen_ref, qon_ref, ke_ref, ko_ref, vt_ref, o_ref,
                m_ref, acc_ref, s0_ref, sa_ref, sb_ref):
    i = pl.program_id(2)
    n_vt = qe_ref.shape[2]
    bq = n_vt * qe_ref.shape[-1]
    bk = bq
    q_cur = (qe_ref, qo_ref)
    q_next = (qen_ref, qon_ref)
    k_refs = (ke_ref, ko_ref)

    m_ref[...] = jnp.full_like(m_ref, -jnp.inf)
    acc_ref[...] = jnp.zeros_like(acc_ref)

    def logits(dst_ref, t, q_refs=q_cur):
        ks = pl.ds(pl.multiple_of(t * bk, bk), bk)
        for hd in range(2):
            qt = jnp.concatenate([q_refs[hd][0, 0, c] for c in range(n_vt)], axis=1)
            dst_ref[hd] = jnp.dot(k_refs[hd][ks, :], qt, preferred_element_type=F32)

    def consume(src_ref, t, masked):
        vt = jnp.concatenate([vt_ref[0, 0, t * n_vt + c] for c in range(n_vt)], axis=1)
        ones = jnp.ones((_FOX_SUM_ROWS, bk), BF16)
        for hd in range(2):
            vaug = jnp.concatenate([vt[hd * FOX_HD:(hd + 1) * FOX_HD], ones], axis=0)
            for c0 in range(0, bq, _FOX_QSUB):
                cs = slice(c0, c0 + _FOX_QSUB)
                s = src_ref[hd, :, cs]
                if masked:
                    k_i = lax.broadcasted_iota(jnp.int32, s.shape, 0)
                    q_i = lax.broadcasted_iota(jnp.int32, s.shape, 1) + c0
                    s = jnp.where(k_i <= q_i, s, -jnp.inf)
                m_old = m_ref[hd, :, cs]
                m_new = jnp.maximum(m_old, jnp.max(s, axis=0, keepdims=True))
                p = jnp.exp2(s - m_new).astype(BF16)
                acc_ref[hd, :, cs] = jnp.exp2(m_old - m_new) * acc_ref[hd, :, cs] + jnp.dot(
                    vaug, p, preferred_element_type=F32)
                m_ref[hd, :, cs] = m_new

    def pair(u, carry=None):
        logits(sb_ref, 2 * u + 2)
        consume(sa_ref, 2 * u + 1, False)
        logits(sa_ref, 2 * u + 3)
        consume(sb_ref, 2 * u + 2, False)
        return carry

    def two_pairs(w, carry=None):
        pair(2 * w)
        pair(2 * w + 1)
        return carry

    def four_pairs(z, carry):
        two_pairs(2 * z)
        two_pairs(2 * z + 1)
        return carry

    @pl.when(i == 0)
    def _():
        logits(s0_ref, 0)
        consume(s0_ref, 0, True)
        logits(s0_ref, 0, q_next)

    n_pairs = (i - 1) // 2

    @pl.when(i > 0)
    def _():
        logits(sa_ref, 1)
        consume(s0_ref, 0, False)
        lax.fori_loop(0, n_pairs // 4, four_pairs, 0)

    @pl.when((i > 0) & (n_pairs % 4 >= 2))
    def _():
        two_pairs(2 * (n_pairs // 4))

    @pl.when((i > 0) & (n_pairs % 2 == 1))
    def _():
        pair(n_pairs - 1)

    @pl.when(i % 2 == 1)
    def _():
        logits(s0_ref, 0, q_next)
        consume(sa_ref, i, True)

    @pl.when((i > 0) & (i % 2 == 0))
    def _():
        logits(sb_ref, i)
        consume(sa_ref, i - 1, False)
        logits(s0_ref, 0, q_next)
        consume(sb_ref, i, True)

    oe = acc_ref[0]
    oo = acc_ref[1]
    ot = jnp.concatenate([oe[:FOX_HD] / oe[FOX_HD:FOX_HD + 1],
                          oo[:FOX_HD] / oo[FOX_HD:FOX_HD + 1]], axis=0)
    o_ref[...] = ot.T.astype(BF16)


def _fox(qe, qo, ke, ko, vt, *, batch, seq):
    n = batch * seq
    bq = FOX_BQ
    tiles = seq // bq
    q_blk = (1, 1, bq // vt.shape[-1]) + vt.shape[3:]
    q_spec = pl.BlockSpec(q_blk, lambda b, j, i: (b, j, i, 0, 0))
    q_next_spec = pl.BlockSpec(q_blk, lambda b, j, i: (b, j, jnp.minimum(i + 1, tiles - 1), 0, 0))
    kv_spec = pl.BlockSpec((seq, LANES), lambda b, j, i: (b, j))
    logit_buf = pltpu.VMEM((2, bq, bq), F32)
    return pl.pallas_call(
        _fox_kernel,
        out_shape=jax.ShapeDtypeStruct((n, FOX_W), BF16),
        grid=(batch, FOX_PAIRS, tiles),
        in_specs=[q_spec, q_spec, q_next_spec, q_next_spec, kv_spec, kv_spec,
                  pl.BlockSpec((1, 1) + vt.shape[2:], lambda b, j, i: (b, j, 0, 0, 0))],
        out_specs=pl.BlockSpec((bq, LANES), lambda b, j, i: (b * tiles + i, j)),
        scratch_shapes=[pltpu.VMEM((2, 1, bq), F32),
                        pltpu.VMEM((2, FOX_HD + _FOX_SUM_ROWS, bq), F32),
                        logit_buf, logit_buf, logit_buf],
        compiler_params=pltpu.CompilerParams(
            dimension_semantics=("arbitrary", "arbitrary", "arbitrary"),
            vmem_limit_bytes=VMEM_LIMIT),
        name="fox",
    )(qe, qo, qe, qo, ke, ko, vt)


def _merge_kernel(x_ref, mod_ref, gpost_ref, oa_ref, ob_ref, sga_ref, sgb_ref,
                  wpa_ref, wpb_ref, wout_ref, o_ref):
    gt = mod_ref[0, 5:6, :]
    for r0 in range(0, x_ref.shape[0], MERGE_SUB):
        rs = pl.ds(r0, MERGE_SUB)
        pa = jnp.dot(oa_ref[rs, :], wpa_ref[...].astype(BF16), preferred_element_type=F32)
        pb = jnp.dot(ob_ref[rs, :], wpb_ref[...].astype(BF16), preferred_element_type=F32)
        merged = sga_ref[rs, :].astype(F32) * pa + sgb_ref[rs, :].astype(F32) * pb
        y = jnp.dot(merged.astype(BF16), wout_ref[...].astype(BF16), preferred_element_type=F32)
        o_ref[rs, :] = x_ref[rs, :] + gt * (_rms(y) * gpost_ref[...])


def _merge(x2d, mods, g_post, oa, ob, sga, sgb, w_pa, w_pb, w_out, *, seq):
    n = x2d.shape[0]
    tm = MERGE_TM
    tiles_per_batch = seq // tm
    row_spec = pl.BlockSpec((tm, D_MODEL), lambda i: (i, 0))
    return pl.pallas_call(
        _merge_kernel,
        out_shape=jax.ShapeDtypeStruct((n, D_MODEL), F32),
        grid=(n // tm,),
        in_specs=[row_spec,
                  pl.BlockSpec((1, N_MOD, D_MODEL), lambda i: (i // tiles_per_batch, 0, 0)),
                  _resident((1, D_MODEL)),
                  row_spec, row_spec, row_spec, row_spec,
                  _resident((D_MODEL, D_MODEL)),
                  _resident((D_MODEL, D_MODEL)),
                  _resident((D_MODEL, D_MODEL))],
        out_specs=row_spec,
        compiler_params=pltpu.CompilerParams(dimension_semantics=("arbitrary",),
                                             vmem_limit_bytes=VMEM_LIMIT),
        name="merge",
    )(x2d, mods, g_post, oa, ob, sga, sgb, w_pa, w_pb, w_out)


def _fox_bias_placement():
    e = np.zeros((LANES, 2 * FOX_W), np.float32)
    for h in range(FOX_HEADS):
        base = LANES * (h // 2) + (FOX_HD if h % 2 == 0 else 0)
        for p in range(3):
            src = _F_LANE0 + FOX_HEADS * p + h
            e[src, base + p] = 1.0
            e[0, base + 3 + p] = 1.0
            e[0, FOX_W + base + p] = 1.0
            e[src, FOX_W + base + 3 + p] = -1.0
    return e


def kernel(x, c, w_ada, b_ada, g_pre, g_post, w_gu1, w_dn1, w_gu2, w_dn2,
           w_in, w_a2, b_a, b_f, g_gla, w_pa, w_pb, w_out):
    batch, seq, d = x.shape
    n = batch * seq
    depth = w_ada.shape[0]
    ltri_np, mask_np = _gla_tables()
    ltri = jnp.asarray(ltri_np, BF16)
    mask = jnp.asarray(mask_np, F32)
    e_qk = jnp.asarray(_fox_bias_placement(), BF16)

    x2d = x.reshape(n, d)
    c_pad = jnp.pad(c, ((0, 16 - batch), (0, 0)))
    for l in range(depth):
        mods = _adaln(c_pad, w_ada[l], b_ada[l][None, :])[:batch].reshape(batch, N_MOD, d)

        x2d = _ffn(x2d, mods, g_pre[l, 0][None, :], g_post[l, 0][None, :],
                   w_gu1[l], w_dn1[l], mod0=0, seq=seq)

        w_all = _regroup(jnp.swapaxes(w_in[l], 0, 1))
        w_a2p = jnp.pad(w_a2[l], ((0, LANES - GLA_RANK), (0, 0))).astype(BF16)
        b_small = jnp.pad(b_f[l], (_F_LANE0, LANES - _F_LANE0 - FOX_HEADS))[None, :]

        (qa, ka, va, la, ra, qe, qo, ke, ko, vt, sga, sgb) = _proj(
            x2d, mods, g_pre[l, 1][None, :], w_all, w_a2p, b_a[l][None, :], b_small, e_qk,
            batch=batch, seq=seq)

        oa = _gla(qa, ka, va, la, ra, g_gla[l].reshape(GLA_HEADS, 1, GLA_HV), ltri, mask,
                  batch=batch, seq=seq)
        ob = _fox(qe, qo, ke, ko, vt, batch=batch, seq=seq)

        x2d = _merge(x2d, mods, g_post[l, 1][None, :], oa, ob, sga, sgb,
                     w_pa[l], w_pb[l], w_out[l], seq=seq)

        x2d = _ffn(x2d, mods, g_pre[l, 2][None, :], g_post[l, 2][None, :],
                   w_gu2[l], w_dn2[l], mod0=6, seq=seq)
    return x2d.reshape(batch, seq, d)
```

```python
import functools

import numpy as np
import jax
import jax.numpy as jnp
from jax import lax
from jax.experimental import pallas as pl
from jax.experimental.pallas import tpu as pltpu

F32 = jnp.float32
BF16 = jnp.bfloat16

EPS = 1e-6
LOG2E = 1.4426950408889634

D_MODEL = 1024
D_FF = 2816
N_MOD = 9
GLA_HEADS = 4
GLA_DK = 512
GLA_DV = 1024
GLA_HK = GLA_DK // GLA_HEADS
GLA_HV = GLA_DV // GLA_HEADS
GLA_RANK = 16
GLA_TAU = 16.0
FOX_HEADS = 16
FOX_HD = 64
FOX_W = FOX_HEADS * FOX_HD
FOX_PAIRS = FOX_HEADS // 2

LANES = 128
VMEM_LIMIT = 56 * 1024 * 1024

FFN_TM = 512
FFN_SUB = 256
FFN_CHUNKS = ((0, 1024), (1024, 1024), (2048, 768))
PROJ_TM = 512
PROJ_SUB = 256
GLA_CHUNK = 128
GLA_TT = 2048
GLA_HPS = 1
GLA_LEVELS = (64, 32, 16, 8, 4, 2, 1)
FOX_BQ = 512
_FOX_SUM_ROWS = 16
_FOX_QSUB = 256
MERGE_TM = 1024
MERGE_SUB = 512

_C_QA, _C_KA, _C_VA, _C_RA = 0, 512, 1024, 2048
_C_QB, _C_KB, _C_VB, _C_GA, _C_GB = 3072, 4096, 5120, 6144, 7168
_C_SMALL = 8192
_W_COLS = 8320
_F_LANE0 = 16


def _resident(shape):
    nd = len(shape)
    return pl.BlockSpec(shape, lambda *_: (0,) * nd, pipeline_mode=pl.Buffered(1))


def _rms(x):
    return x * lax.rsqrt(jnp.mean(x * x, axis=-1, keepdims=True) + EPS)


def _log_sigmoid(x):
    return jnp.minimum(x, 0.0) - jnp.log(1.0 + jnp.exp(-jnp.abs(x)))


def _adaln_kernel(c_ref, w_ref, b_ref, o_ref):
    c = c_ref[...]
    s = c * jax.nn.sigmoid(c)
    o_ref[...] = jnp.dot(s.astype(BF16), w_ref[...].astype(BF16),
                         preferred_element_type=F32) + b_ref[...]


def _adaln(c_pad, w_ada, b_ada):
    rows = c_pad.shape[0]
    ncol = w_ada.shape[1]
    tn = 1024
    return pl.pallas_call(
        _adaln_kernel,
        out_shape=jax.ShapeDtypeStruct((rows, ncol), F32),
        grid=(ncol // tn,),
        in_specs=[pl.BlockSpec((rows, D_MODEL), lambda j: (0, 0)),
                  pl.BlockSpec((D_MODEL, tn), lambda j: (0, j)),
                  pl.BlockSpec((1, tn), lambda j: (0, j))],
        out_specs=pl.BlockSpec((rows, tn), lambda j: (0, j)),
        compiler_params=pltpu.CompilerParams(dimension_semantics=("arbitrary",)),
        name="adaln",
    )(c_pad, w_ada, b_ada)


def _ffn_kernel(x_ref, mod_ref, gpre_ref, gpost_ref, wgu_ref, wdn_ref, o_ref, *, mod0):
    sh = mod_ref[0, mod0:mod0 + 1, :]
    sc = mod_ref[0, mod0 + 1:mod0 + 2, :]
    gt = mod_ref[0, mod0 + 2:mod0 + 3, :]
    for r0 in range(0, x_ref.shape[0], FFN_SUB):
        rs = pl.ds(r0, FFN_SUB)
        x = x_ref[rs, :]
        h = (_rms(x) * gpre_ref[...]) * (1.0 + sc) + sh
        hb = h.astype(BF16)
        acc = None
        for c0, cw in FFN_CHUNKS:
            g = jnp.dot(hb, wgu_ref[:, c0:c0 + cw].astype(BF16), preferred_element_type=F32)
            u = jnp.dot(hb, wgu_ref[:, D_FF + c0:D_FF + c0 + cw].astype(BF16),
                        preferred_element_type=F32)
            a = (g * jax.nn.sigmoid(g) * u).astype(BF16)
            part = jnp.dot(a, wdn_ref[c0:c0 + cw, :].astype(BF16), preferred_element_type=F32)
            acc = part if acc is None else acc + part
        o_ref[rs, :] = x + (0.5 * gt) * (_rms(acc) * gpost_ref[...])


def _ffn(x2d, mods, g_pre, g_post, w_gu, w_dn, *, mod0, seq):
    n = x2d.shape[0]
    tiles_per_batch = seq // FFN_TM
    return pl.pallas_call(
        functools.partial(_ffn_kernel, mod0=mod0),
        out_shape=jax.ShapeDtypeStruct((n, D_MODEL), F32),
        grid=(n // FFN_TM,),
        in_specs=[pl.BlockSpec((FFN_TM, D_MODEL), lambda i: (i, 0)),
                  pl.BlockSpec((1, N_MOD, D_MODEL), lambda i: (i // tiles_per_batch, 0, 0)),
                  _resident((1, D_MODEL)),
                  _resident((1, D_MODEL)),
                  _resident((D_MODEL, 2 * D_FF)),
                  _resident((D_FF, D_MODEL))],
        out_specs=pl.BlockSpec((FFN_TM, D_MODEL), lambda i: (i, 0)),
        compiler_params=pltpu.CompilerParams(dimension_semantics=("arbitrary",),
                                             vmem_limit_bytes=VMEM_LIMIT),
        name="ffn",
    )(x2d, mods, g_pre, g_post, w_gu, w_dn)


_S_ALOW = 2 * GLA_DK + GLA_DV
_S_WIDE1 = _S_ALOW + GLA_RANK
_S_FB = _S_WIDE1 + GLA_DV + 3 * FOX_W
_S_WIDE2 = _S_FB + FOX_HEADS
_S_END = _S_WIDE2 + 2 * D_MODEL
_REGROUP_TR = 128


def _regroup_kernel(wt_ref, o_ref):
    def blk(src):
        return wt_ref[src:src + LANES, :].T

    def put(dst, src, width):
        for c in range(0, width, LANES):
            o_ref[:, dst + c:dst + c + LANES] = blk(src + c).astype(BF16)

    put(_C_QA, 0, _S_ALOW)
    put(_C_RA, _S_WIDE1, _S_FB - _S_WIDE1)
    put(_C_GA, _S_WIDE2, _S_END - _S_WIDE2)
    a_blk = blk(_S_ALOW)
    f_blk = blk(_S_FB - _F_LANE0)
    lane = lax.broadcasted_iota(jnp.int32, a_blk.shape, 1)
    small = jnp.where(lane < GLA_RANK, a_blk,
                      jnp.where(lane < _F_LANE0 + FOX_HEADS, f_blk, 0.0))
    o_ref[:, _C_SMALL:_W_COLS] = small.astype(BF16)


def _regroup(w_in_t):
    cols, rows = w_in_t.shape
    assert cols == _S_END and (_S_FB - _F_LANE0) % LANES == 0 and _F_LANE0 == GLA_RANK
    return pl.pallas_call(
        _regroup_kernel,
        out_shape=jax.ShapeDtypeStruct((rows, _W_COLS), BF16),
        grid=(rows // _REGROUP_TR,),
        in_specs=[pl.BlockSpec((cols, _REGROUP_TR), lambda i: (0, i))],
        out_specs=pl.BlockSpec((_REGROUP_TR, _W_COLS), lambda i: (i, 0)),
        compiler_params=pltpu.CompilerParams(dimension_semantics=("arbitrary",)),
        name="regroup",
    )(w_in_t)


def _proj_kernel(x_ref, mod_ref, gpre_ref, w_ref, wa2_ref, ba_ref, bsm_ref, eqk_ref,
                 qa_ref, ka_ref, va_ref, la_ref, ra_ref,
                 qe_ref, qo_ref, ke_ref, ko_ref, vt_ref, sga_ref, sgb_ref,
                 fcar_ref, *, tiles_per_batch):
    i = pl.program_id(0)

    @pl.when(i % tiles_per_batch == 0)
    def _():
        fcar_ref[...] = jnp.zeros_like(fcar_ref)

    for r0 in range(0, x_ref.shape[0], PROJ_SUB):
        _proj_rows(r0, x_ref, mod_ref, gpre_ref, w_ref, wa2_ref, ba_ref, bsm_ref, eqk_ref,
                   qa_ref, ka_ref, va_ref, la_ref, ra_ref,
                   qe_ref, qo_ref, ke_ref, ko_ref, vt_ref, sga_ref, sgb_ref, fcar_ref)


def _proj_rows(r0, x_ref, mod_ref, gpre_ref, w_ref, wa2_ref, ba_ref, bsm_ref, eqk_ref,
               qa_ref, ka_ref, va_ref, la_ref, ra_ref,
               qe_ref, qo_ref, ke_ref, ko_ref, vt_ref, sga_ref, sgb_ref, fcar_ref):
    tm = PROJ_SUB
    rs = pl.ds(r0, tm)
    cs = slice(r0, r0 + tm)
    x = x_ref[rs, :]
    sh = mod_ref[0, 3:4, :]
    sc = mod_ref[0, 4:5, :]
    hb = ((_rms(x) * gpre_ref[...]) * (1.0 + sc) + sh).astype(BF16)

    def proj(c0, cw):
        return jnp.dot(hb, w_ref[:, c0:c0 + cw], preferred_element_type=F32)

    qa_ref[rs, :] = proj(_C_QA, GLA_DK) * (GLA_HK ** -0.5)
    ka_ref[rs, :] = proj(_C_KA, GLA_DK)
    va_ref[rs, :] = proj(_C_VA, GLA_DV).astype(BF16)
    r = proj(_C_RA, GLA_DV)
    ra_ref[rs, :] = (r * jax.nn.sigmoid(r)).astype(BF16)

    zs = proj(_C_SMALL, LANES)
    xa = jnp.dot(zs.astype(BF16), wa2_ref[...], preferred_element_type=F32) + ba_ref[...]
    la_ref[rs, :] = _log_sigmoid(xa) * (1.0 / GLA_TAU)

    lane = lax.broadcasted_iota(jnp.int32, (tm, LANES), 1)
    row = lax.broadcasted_iota(jnp.int32, (tm, LANES), 0)
    in_f = (lane >= _F_LANE0) & (lane < _F_LANE0 + FOX_HEADS)
    f = jnp.where(in_f, _log_sigmoid(zs + bsm_ref[...]), 0.0)
    shift = 1
    while shift < tm:
        f = f + jnp.where(row >= shift, pltpu.roll(f, shift, 0), 0.0)
        shift *= 2
    f = f + fcar_ref[...]
    fcar_ref[...] = f[tm - 1:tm, :]

    f2 = f * LOG2E
    p0 = f2.astype(BF16).astype(F32)
    r1 = f2 - p0
    p1 = r1.astype(BF16).astype(F32)
    p2 = (r1 - p1).astype(BF16).astype(F32)
    fc = p0 + pltpu.roll(p1, FOX_HEADS, 1) + pltpu.roll(p2, 2 * FOX_HEADS, 1)
    fc = jnp.where(lane == 0, 1.0, fc)
    aug = jnp.dot(fc.astype(BF16), eqk_ref[...], preferred_element_type=F32)
    augq = aug[:, :FOX_W]
    augk = aug[:, FOX_W:]

    lane_w = lax.broadcasted_iota(jnp.int32, (tm, FOX_W), 1)
    low_half = (lane_w & (LANES - 1)) < FOX_HD
    zq = proj(_C_QB, FOX_W) * (FOX_HD ** -0.5 * LOG2E)
    q_even = jnp.where(low_half, zq, augq)
    q_odd = jnp.where(low_half, augq, zq)
    for j in range(FOX_PAIRS):
        qe_ref[0, j, 0, :, cs] = q_even[:, j * LANES:(j + 1) * LANES].T.astype(BF16)
        qo_ref[0, j, 0, :, cs] = q_odd[:, j * LANES:(j + 1) * LANES].T.astype(BF16)
    zk = proj(_C_KB, FOX_W)
    ke_ref[rs, :] = jnp.where(low_half, zk, augk).astype(BF16)
    ko_ref[rs, :] = jnp.where(low_half, augk, zk).astype(BF16)
    zv = proj(_C_VB, FOX_W)
    for j in range(FOX_PAIRS):
        vt_ref[0, j, 0, :, cs] = zv[:, j * LANES:(j + 1) * LANES].T.astype(BF16)
    sga_ref[rs, :] = jax.nn.sigmoid(proj(_C_GA, D_MODEL)).astype(BF16)
    sgb_ref[rs, :] = jax.nn.sigmoid(proj(_C_GB, D_MODEL)).astype(BF16)


def _proj(x2d, mods, g_pre, w_all, w_a2p, b_a, b_small, e_qk, *, batch, seq):
    n = x2d.shape[0]
    tm = PROJ_TM
    tiles_per_batch = seq // tm
    row_spec = lambda w: pl.BlockSpec((tm, w), lambda i: (i, 0))
    transposed = jax.ShapeDtypeStruct((batch, FOX_PAIRS, tiles_per_batch, LANES, tm), BF16)
    out_shapes = [
        jax.ShapeDtypeStruct((n, GLA_DK), F32),
        jax.ShapeDtypeStruct((n, GLA_DK), F32),
        jax.ShapeDtypeStruct((n, GLA_DV), BF16),
        jax.ShapeDtypeStruct((n, GLA_DK), F32),
        jax.ShapeDtypeStruct((n, GLA_DV), BF16),
        transposed,
        transposed,
        jax.ShapeDtypeStruct((n, FOX_W), BF16),
        jax.ShapeDtypeStruct((n, FOX_W), BF16),
        transposed,
        jax.ShapeDtypeStruct((n, D_MODEL), BF16),
        jax.ShapeDtypeStruct((n, D_MODEL), BF16),
    ]
    return pl.pallas_call(
        functools.partial(_proj_kernel, tiles_per_batch=tiles_per_batch),
        out_shape=out_shapes,
        grid=(n // tm,),
        in_specs=[row_spec(D_MODEL),
                  pl.BlockSpec((1, N_MOD, D_MODEL), lambda i: (i // tiles_per_batch, 0, 0)),
                  _resident((1, D_MODEL)),
                  _resident((D_MODEL, _W_COLS)),
                  _resident((LANES, GLA_DK)),
                  _resident((1, GLA_DK)),
                  _resident((1, LANES)),
                  _resident((LANES, 2 * FOX_W))],
        out_specs=[row_spec(s.shape[1]) if len(s.shape) == 2 else
                   pl.BlockSpec((1, FOX_PAIRS, 1, LANES, tm),
                                lambda i: (i // tiles_per_batch, 0, i % tiles_per_batch, 0, 0))
                   for s in out_shapes],
        scratch_shapes=[pltpu.VMEM((1, LANES), F32)],
        compiler_params=pltpu.CompilerParams(dimension_semantics=("arbitrary",),
                                             vmem_limit_bytes=VMEM_LIMIT),
        name="proj",
    )(x2d, mods, g_pre, w_all, w_a2p, b_a, b_small, e_qk)


def _gla_tables():
    c = GLA_CHUNK
    t = np.arange(c)[:, None]
    u = np.arange(c)[None, :]
    masks = [(t == u)]
    for s in GLA_LEVELS:
        same = (t // (2 * s)) == (u // (2 * s))
        masks.append(same & (t % (2 * s) >= s) & (u % (2 * s) < s))
    ltri = (u <= t).astype(np.float32)
    mask = np.stack(masks, axis=0).astype(np.float32)
    return ltri, mask


def _dot_nt(a, b):
    return lax.dot_general(a, b, (((1,), (1,)), ((), ())), preferred_element_type=F32)


def _gla_pivots(b, b_rows_ref, s):
    c, dk = b.shape
    if s >= 8:
        return jnp.concatenate(
            [jnp.broadcast_to(b_rows_ref[pl.ds(g0 + s, 1), :], (2 * s, dk))
             for g0 in range(0, c, 2 * s)], axis=0)
    b3 = b.reshape(c // 8, 8, dk)
    sub = lax.broadcasted_iota(jnp.int32, b3.shape, 1)
    offset = s - (sub & (2 * s - 1))
    piv = b3
    for d in range(-(s - 1), s + 1):
        if d != 0:
            piv = jnp.where(offset == d, pltpu.roll(b3, (-d) % 8, 1), piv)
    return piv.reshape(c, dk)


def _gla_kernel(q_ref, k_ref, v_ref, g_ref, r_ref, gg_ref, ltri_ref, mask_ref, o_ref, s_ref, b_ref):
    c = GLA_CHUNK

    @pl.when(pl.program_id(2) == 0)
    def _():
        s_ref[...] = jnp.zeros_like(s_ref)

    n_chunks = q_ref.shape[0] // c
    states = [s_ref[hd] for hd in range(GLA_HPS)]
    for ci in range(n_chunks):
        sl = pl.ds(ci * c, c)
        for hd in range(GLA_HPS):
            ks = slice(hd * GLA_HK, (hd + 1) * GLA_HK)
            vs = slice(hd * GLA_HV, (hd + 1) * GLA_HV)
            slot = ci * GLA_HPS + hd
            q = q_ref[sl, ks]
            k = k_ref[sl, ks]
            g = g_ref[sl, ks]
            v = v_ref[sl, vs]
            g_hi = g.astype(BF16)
            g_lo = (g - g_hi.astype(F32)).astype(BF16)
            x2 = jnp.dot(ltri_ref[...], jnp.concatenate([g_hi, g_lo], axis=1),
                         preferred_element_type=F32)
            b = (x2[:, :GLA_HK] + x2[:, GLA_HK:]) * LOG2E
            b_ref[slot] = b
            eb = jnp.exp2(b)
            ebl = jnp.exp2(b_ref[slot, pl.ds(c - 1, 1), :] - b)

            a = mask_ref[0] * _dot_nt(q.astype(BF16), k.astype(BF16))
            for li, s in enumerate(GLA_LEVELS):
                f = jnp.exp2(-jnp.abs(b - _gla_pivots(b, b_ref.at[slot], s)))
                a = a + mask_ref[li + 1] * _dot_nt((q * f).astype(BF16), (k * f).astype(BF16))

            lhs = jnp.concatenate([(q * eb).astype(BF16), a.astype(BF16)], axis=1)
            rhs = jnp.concatenate([states[hd].astype(BF16), v], axis=0)
            o = jnp.dot(lhs, rhs, preferred_element_type=F32)

            upd = jnp.dot((k * ebl).T.astype(BF16), v, preferred_element_type=F32)
            decay = eb.T[:, c - 1:c]
            states[hd] = decay * states[hd] + upd

            on = _rms(o) * gg_ref[hd]
            o_ref[sl, vs] = (on * r_ref[sl, vs].astype(F32)).astype(BF16)
    for hd in range(GLA_HPS):
        s_ref[hd] = states[hd]


def _gla(qa, ka, va, la, ra, g_gla, ltri, mask, *, batch, seq):
    n = qa.shape[0]
    tiles = seq // GLA_TT
    rows = lambda b, h, t: b * tiles + t
    return pl.pallas_call(
        _gla_kernel,
        out_shape=jax.ShapeDtypeStruct((n, GLA_DV), BF16),
        grid=(batch, GLA_HEADS // GLA_HPS, tiles),
        in_specs=[pl.BlockSpec((GLA_TT, GLA_HPS * GLA_HK), lambda b, h, t: (rows(b, h, t), h)),
                  pl.BlockSpec((GLA_TT, GLA_HPS * GLA_HK), lambda b, h, t: (rows(b, h, t), h)),
                  pl.BlockSpec((GLA_TT, GLA_HPS * GLA_HV), lambda b, h, t: (rows(b, h, t), h)),
                  pl.BlockSpec((GLA_TT, GLA_HPS * GLA_HK), lambda b, h, t: (rows(b, h, t), h)),
                  pl.BlockSpec((GLA_TT, GLA_HPS * GLA_HV), lambda b, h, t: (rows(b, h, t), h)),
                  pl.BlockSpec((GLA_HPS, 1, GLA_HV), lambda b, h, t: (h, 0, 0)),
                  _resident(ltri.shape),
                  _resident(mask.shape)],
        out_specs=pl.BlockSpec((GLA_TT, GLA_HPS * GLA_HV), lambda b, h, t: (rows(b, h, t), h)),
        scratch_shapes=[pltpu.VMEM((GLA_HPS, GLA_HK, GLA_HV), F32),
                        pltpu.VMEM((GLA_HPS * GLA_TT // GLA_CHUNK, GLA_CHUNK, GLA_HK), F32)],
        compiler_params=pltpu.CompilerParams(
            dimension_semantics=("arbitrary", "arbitrary", "arbitrary"),
            vmem_limit_bytes=VMEM_LIMIT),
        name="gla",
    )(qa, ka, va, la, ra, g_gla, ltri, mask)


def _fox_kernel(qe_ref, qo_ref, qen_ref, qon_ref, ke_ref, ko_ref, vt_ref, o_ref,
                m_ref, acc_ref, s0_ref, sa_ref, sb_ref):
    i = pl.program_id(2)
    n_vt = qe_ref.shape[2]
    bq = n_vt * qe_ref.shape[-1]
    bk = bq
    q_cur = (qe_ref, qo_ref)
    q_next = (qen_ref, qon_ref)
    k_refs = (ke_ref, ko_ref)

    m_ref[...] = jnp.full_like(m_ref, -jnp.inf)
    acc_ref[...] = jnp.zeros_like(acc_ref)

    def logits(dst_ref, t, q_refs=q_cur):
        ks = pl.ds(pl.multiple_of(t * bk, bk), bk)
        for hd in range(2):
            qt = jnp.concatenate([q_refs[hd][0, 0, c] for c in range(n_vt)], axis=1)
            dst_ref[hd] = jnp.dot(k_refs[hd][ks, :], qt, preferred_element_type=F32)

    def consume(src_ref, t, masked):
        vt = jnp.concatenate([vt_ref[0, 0, t * n_vt + c] for c in range(n_vt)], axis=1)
        ones = jnp.ones((_FOX_SUM_ROWS, bk), BF16)
        for hd in range(2):
            vaug = jnp.concatenate([vt[hd * FOX_HD:(hd + 1) * FOX_HD], ones], axis=0)
            for c0 in range(0, bq, _FOX_QSUB):
                cs = slice(c0, c0 + _FOX_QSUB)
                s = src_ref[hd, :, cs]
                if masked:
                    k_i = lax.broadcasted_iota(jnp.int32, s.shape, 0)
                    q_i = lax.broadcasted_iota(jnp.int32, s.shape, 1) + c0
                    s = jnp.where(k_i <= q_i, s, -jnp.inf)
                m_old = m_ref[hd, :, cs]
                m_new = jnp.maximum(m_old, jnp.max(s, axis=0, keepdims=True))
                p = jnp.exp2(s - m_new).astype(BF16)
                acc_ref[hd, :, cs] = jnp.exp2(m_old - m_new) * acc_ref[hd, :, cs] + jnp.dot(
                    vaug, p, preferred_element_type=F32)
                m_ref[hd, :, cs] = m_new

    def pair(u, carry=None):
        logits(sb_ref, 2 * u + 2)
        consume(sa_ref, 2 * u + 1, False)
        logits(sa_ref, 2 * u + 3)
        consume(sb_ref, 2 * u + 2, False)
        return carry

    def two_pairs(w, carry=None):
        pair(2 * w)
        pair(2 * w + 1)
        return carry

    def four_pairs(z, carry):
        two_pairs(2 * z)
        two_pairs(2 * z + 1)
        return carry

    @pl.when(i == 0)
    def _():
        logits(s0_ref, 0)
        consume(s0_ref, 0, True)
        logits(s0_ref, 0, q_next)

    n_pairs = (i - 1) // 2

    @pl.when(i > 0)
    def _():
        logits(sa_ref, 1)
        consume(s0_ref, 0, False)
        lax.fori_loop(0, n_pairs // 4, four_pairs, 0)

    @pl.when((i > 0) & (n_pairs % 4 >= 2))
    def _():
        two_pairs(2 * (n_pairs // 4))

    @pl.when((i > 0) & (n_pairs % 2 == 1))
    def _():
        pair(n_pairs - 1)

    @pl.when(i % 2 == 1)
    def _():
        logits(s0_ref, 0, q_next)
        consume(sa_ref, i, True)

    @pl.when((i > 0) & (i % 2 == 0))
    def _():
        logits(sb_ref, i)
        consume(sa_ref, i - 1, False)
        logits(s0_ref, 0, q_next)
        consume(sb_ref, i, True)

    oe = acc_ref[0]
    oo = acc_ref[1]
    ot = jnp.concatenate([oe[:FOX_HD] / oe[FOX_HD:FOX_HD + 1],
                          oo[:FOX_HD] / oo[FOX_HD:FOX_HD + 1]], axis=0)
    o_ref[...] = ot.T.astype(BF16)


def _fox(qe, qo, ke, ko, vt, *, batch, seq):
    n = batch * seq
    bq = FOX_BQ
    tiles = seq // bq
    q_blk = (1, 1, bq // vt.shape[-1]) + vt.shape[3:]
    q_spec = pl.BlockSpec(q_blk, lambda b, j, i: (b, j, i, 0, 0))
    q_next_spec = pl.BlockSpec(q_blk, lambda b, j, i: (b, j, jnp.minimum(i + 1, tiles - 1), 0, 0))
    kv_spec = pl.BlockSpec((seq, LANES), lambda b, j, i: (b, j))
    logit_buf = pltpu.VMEM((2, bq, bq), F32)
    return pl.pallas_call(
        _fox_kernel,
        out_shape=jax.ShapeDtypeStruct((n, FOX_W), BF16),
        grid=(batch, FOX_PAIRS, tiles),
        in_specs=[q_spec, q_spec, q_next_spec, q_next_spec, kv_spec, kv_spec,
                  pl.BlockSpec((1, 1) + vt.shape[2:], lambda b, j, i: (b, j, 0, 0, 0))],
        out_specs=pl.BlockSpec((bq, LANES), lambda b, j, i: (b * tiles + i, j)),
        scratch_shapes=[pltpu.VMEM((2, 1, bq), F32),
                        pltpu.VMEM((2, FOX_HD + _FOX_SUM_ROWS, bq), F32),
                        logit_buf, logit_buf, logit_buf],
        compiler_params=pltpu.CompilerParams(
            dimension_semantics=("arbitrary", "arbitrary", "arbitrary"),
            vmem_limit_bytes=VMEM_LIMIT),
        name="fox",
    )(qe, qo, qe, qo, ke, ko, vt)


def _merge_kernel(x_ref, mod_ref, gpost_ref, oa_ref, ob_ref, sga_ref, sgb_ref,
                  wpa_ref, wpb_ref, wout_ref, o_ref):
    gt = mod_ref[0, 5:6, :]
    for r0 in range(0, x_ref.shape[0], MERGE_SUB):
        rs = pl.ds(r0, MERGE_SUB)
        pa = jnp.dot(oa_ref[rs, :], wpa_ref[...].astype(BF16), preferred_element_type=F32)
        pb = jnp.dot(ob_ref[rs, :], wpb_ref[...].astype(BF16), preferred_element_type=F32)
        merged = sga_ref[rs, :].astype(F32) * pa + sgb_ref[rs, :].astype(F32) * pb
        y = jnp.dot(merged.astype(BF16), wout_ref[...].astype(BF16), preferred_element_type=F32)
        o_ref[rs, :] = x_ref[rs, :] + gt * (_rms(y) * gpost_ref[...])


def _merge(x2d, mods, g_post, oa, ob, sga, sgb, w_pa, w_pb, w_out, *, seq):
    n = x2d.shape[0]
    tm = MERGE_TM
    tiles_per_batch = seq // tm
    row_spec = pl.BlockSpec((tm, D_MODEL), lambda i: (i, 0))
    return pl.pallas_call(
        _merge_kernel,
        out_shape=jax.ShapeDtypeStruct((n, D_MODEL), F32),
        grid=(n // tm,),
        in_specs=[row_spec,
                  pl.BlockSpec((1, N_MOD, D_MODEL), lambda i: (i // tiles_per_batch, 0, 0)),
                  _resident((1, D_MODEL)),
                  row_spec, row_spec, row_spec, row_spec,
                  _resident((D_MODEL, D_MODEL)),
                  _resident((D_MODEL, D_MODEL)),
                  _resident((D_MODEL, D_MODEL))],
        out_specs=row_spec,
        compiler_params=pltpu.CompilerParams(dimension_semantics=("arbitrary",),
                                             vmem_limit_bytes=VMEM_LIMIT),
        name="merge",
    )(x2d, mods, g_post, oa, ob, sga, sgb, w_pa, w_pb, w_out)


def _fox_bias_placement():
    e = np.zeros((LANES, 2 * FOX_W), np.float32)
    for h in range(FOX_HEADS):
        base = LANES * (h // 2) + (FOX_HD if h % 2 == 0 else 0)
        for p in range(3):
            src = _F_LANE0 + FOX_HEADS * p + h
            e[src, base + p] = 1.0
            e[0, base + 3 + p] = 1.0
            e[0, FOX_W + base + p] = 1.0
            e[src, FOX_W + base + 3 + p] = -1.0
    return e


def kernel(x, c, w_ada, b_ada, g_pre, g_post, w_gu1, w_dn1, w_gu2, w_dn2,
           w_in, w_a2, b_a, b_f, g_gla, w_pa, w_pb, w_out):
    batch, seq, d = x.shape
    n = batch * seq
    depth = w_ada.shape[0]
    ltri_np, mask_np = _gla_tables()
    ltri = jnp.asarray(ltri_np, BF16)
    mask = jnp.asarray(mask_np, F32)
    e_qk = jnp.asarray(_fox_bias_placement(), BF16)

    x2d = x.reshape(n, d)
    c_pad = jnp.pad(c, ((0, 16 - batch), (0, 0)))
    for l in range(depth):
        mods = _adaln(c_pad, w_ada[l], b_ada[l][None, :])[:batch].reshape(batch, N_MOD, d)

        x2d = _ffn(x2d, mods, g_pre[l, 0][None, :], g_post[l, 0][None, :],
                   w_gu1[l], w_dn1[l], mod0=0, seq=seq)

        w_all = _regroup(jnp.swapaxes(w_in[l], 0, 1))
        w_a2p = jnp.pad(w_a2[l], ((0, LANES - GLA_RANK), (0, 0))).astype(BF16)
        b_small = jnp.pad(b_f[l], (_F_LANE0, LANES - _F_LANE0 - FOX_HEADS))[None, :]

        (qa, ka, va, la, ra, qe, qo, ke, ko, vt, sga, sgb) = _proj(
            x2d, mods, g_pre[l, 1][None, :], w_all, w_a2p, b_a[l][None, :], b_small, e_qk,
            batch=batch, seq=seq)

        oa = _gla(qa, ka, va, la, ra, g_gla[l].reshape(GLA_HEADS, 1, GLA_HV), ltri, mask,
                  batch=batch, seq=seq)
        ob = _fox(qe, qo, ke, ko, vt, batch=batch, seq=seq)

        x2d = _merge(x2d, mods, g_post[l, 1][None, :], oa, ob, sga, sgb,
                     w_pa[l], w_pb[l], w_out[l], seq=seq)

        x2d = _ffn(x2d, mods, g_pre[l, 2][None, :], g_post[l, 2][None, :],
                   w_gu2[l], w_dn2[l], mod0=6, seq=seq)
    return x2d.reshape(batch, seq, d)
```

```python
import functools

import numpy as np
import jax
import jax.numpy as jnp
from jax import lax
from jax.experimental import pallas as pl
from jax.experimental.pallas import tpu as pltpu

F32 = jnp.float32
BF16 = jnp.bfloat16

EPS = 1e-6
LOG2E = 1.4426950408889634

D_MODEL = 1024
D_FF = 2816
N_MOD = 9
GLA_HEADS = 4
GLA_DK = 512
GLA_DV = 1024
GLA_HK = GLA_DK // GLA_HEADS
GLA_HV = GLA_DV // GLA_HEADS
GLA_RANK = 16
GLA_TAU = 16.0
FOX_HEADS = 16
FOX_HD = 64
FOX_W = FOX_HEADS * FOX_HD
FOX_PAIRS = FOX_HEADS // 2

LANES = 128
VMEM_LIMIT = 56 * 1024 * 1024

FFN_TM = 512
FFN_SUB = 256
FFN_CHUNKS = ((0, 1024), (1024, 1024), (2048, 768))
PROJ_TM = 512
PROJ_SUB = 256
GLA_CHUNK = 128
GLA_TT = 2048
GLA_HPS = 1
GLA_LEVELS = (64, 32, 16, 8, 4, 2, 1)
FOX_BQ = 512
_FOX_SUM_ROWS = 16
_FOX_QSUB = 256
MERGE_TM = 1024
MERGE_SUB = 512

_C_QA, _C_KA, _C_VA, _C_RA = 0, 512, 1024, 2048
_C_QB, _C_KB, _C_VB, _C_GA, _C_GB = 3072, 4096, 5120, 6144, 7168
_C_SMALL = 8192
_W_COLS = 8320
_F_LANE0 = 16


def _resident(shape):
    nd = len(shape)
    return pl.BlockSpec(shape, lambda *_: (0,) * nd, pipeline_mode=pl.Buffered(1))


def _rms(x):
    return x * lax.rsqrt(jnp.mean(x * x, axis=-1, keepdims=True) + EPS)


def _log_sigmoid(x):
    return jnp.minimum(x, 0.0) - jnp.log(1.0 + jnp.exp(-jnp.abs(x)))


def _adaln_kernel(c_ref, w_ref, b_ref, o_ref):
    c = c_ref[...]
    s = c * jax.nn.sigmoid(c)
    o_ref[...] = jnp.dot(s.astype(BF16), w_ref[...].astype(BF16),
                         preferred_element_type=F32) + b_ref[...]


def _adaln(c_pad, w_ada, b_ada):
    rows = c_pad.shape[0]
    ncol = w_ada.shape[1]
    tn = 1024
    return pl.pallas_call(
        _adaln_kernel,
        out_shape=jax.ShapeDtypeStruct((rows, ncol), F32),
        grid=(ncol // tn,),
        in_specs=[pl.BlockSpec((rows, D_MODEL), lambda j: (0, 0)),
                  pl.BlockSpec((D_MODEL, tn), lambda j: (0, j)),
                  pl.BlockSpec((1, tn), lambda j: (0, j))],
        out_specs=pl.BlockSpec((rows, tn), lambda j: (0, j)),
        compiler_params=pltpu.CompilerParams(dimension_semantics=("arbitrary",)),
        name="adaln",
    )(c_pad, w_ada, b_ada)


def _ffn_kernel(x_ref, mod_ref, gpre_ref, gpost_ref, wgu_ref, wdn_ref, o_ref, *, mod0):
    sh = mod_ref[0, mod0:mod0 + 1, :]
    sc = mod_ref[0, mod0 + 1:mod0 + 2, :]
    gt = mod_ref[0, mod0 + 2:mod0 + 3, :]
    for r0 in range(0, x_ref.shape[0], FFN_SUB):
        rs = pl.ds(r0, FFN_SUB)
        x = x_ref[rs, :]
        h = (_rms(x) * gpre_ref[...]) * (1.0 + sc) + sh
        hb = h.astype(BF16)
        acc = None
        for c0, cw in FFN_CHUNKS:
            g = jnp.dot(hb, wgu_ref[:, c0:c0 + cw].astype(BF16), preferred_element_type=F32)
            u = jnp.dot(hb, wgu_ref[:, D_FF + c0:D_FF + c0 + cw].astype(BF16),
                        preferred_element_type=F32)
            a = (g * jax.nn.sigmoid(g) * u).astype(BF16)
            part = jnp.dot(a, wdn_ref[c0:c0 + cw, :].astype(BF16), preferred_element_type=F32)
            acc = part if acc is None else acc + part
        o_ref[rs, :] = x + (0.5 * gt) * (_rms(acc) * gpost_ref[...])


def _ffn(x2d, mods, g_pre, g_post, w_gu, w_dn, *, mod0, seq):
    n = x2d.shape[0]
    tiles_per_batch = seq // FFN_TM
    return pl.pallas_call(
        functools.partial(_ffn_kernel, mod0=mod0),
        out_shape=jax.ShapeDtypeStruct((n, D_MODEL), F32),
        grid=(n // FFN_TM,),
        in_specs=[pl.BlockSpec((FFN_TM, D_MODEL), lambda i: (i, 0)),
                  pl.BlockSpec((1, N_MOD, D_MODEL), lambda i: (i // tiles_per_batch, 0, 0)),
                  _resident((1, D_MODEL)),
                  _resident((1, D_MODEL)),
                  _resident((D_MODEL, 2 * D_FF)),
                  _resident((D_FF, D_MODEL))],
        out_specs=pl.BlockSpec((FFN_TM, D_MODEL), lambda i: (i, 0)),
        compiler_params=pltpu.CompilerParams(dimension_semantics=("arbitrary",),
                                             vmem_limit_bytes=VMEM_LIMIT),
        name="ffn",
    )(x2d, mods, g_pre, g_post, w_gu, w_dn)


_S_ALOW = 2 * GLA_DK + GLA_DV
_S_WIDE1 = _S_ALOW + GLA_RANK
_S_FB = _S_WIDE1 + GLA_DV + 3 * FOX_W
_S_WIDE2 = _S_FB + FOX_HEADS
_S_END = _S_WIDE2 + 2 * D_MODEL
_REGROUP_TR = 256


def _regroup_kernel(wt_ref, o_ref):
    def blk(src):
        return wt_ref[src:src + LANES, :].T

    def put(dst, src, width):
        for c in range(0, width, LANES):
            o_ref[:, dst + c:dst + c + LANES] = blk(src + c).astype(BF16)

    put(_C_QA, 0, _S_ALOW)
    put(_C_RA, _S_WIDE1, _S_FB - _S_WIDE1)
    put(_C_GA, _S_WIDE2, _S_END - _S_WIDE2)
    a_blk = blk(_S_ALOW)
    f_blk = blk(_S_FB - _F_LANE0)
    lane = lax.broadcasted_iota(jnp.int32, a_blk.shape, 1)
    small = jnp.where(lane < GLA_RANK, a_blk,
                      jnp.where(lane < _F_LANE0 + FOX_HEADS, f_blk, 0.0))
    o_ref[:, _C_SMALL:_W_COLS] = small.astype(BF16)


def _regroup(w_in_t):
    cols, rows = w_in_t.shape
    assert cols == _S_END and (_S_FB - _F_LANE0) % LANES == 0 and _F_LANE0 == GLA_RANK
    return pl.pallas_call(
        _regroup_kernel,
        out_shape=jax.ShapeDtypeStruct((rows, _W_COLS), BF16),
        grid=(rows // _REGROUP_TR,),
        in_specs=[pl.BlockSpec((cols, _REGROUP_TR), lambda i: (0, i))],
        out_specs=pl.BlockSpec((_REGROUP_TR, _W_COLS), lambda i: (i, 0)),
        compiler_params=pltpu.CompilerParams(dimension_semantics=("arbitrary",)),
        name="regroup",
    )(w_in_t)


def _proj_kernel(x_ref, mod_ref, gpre_ref, w_ref, wa2_ref, ba_ref, bsm_ref, eqk_ref,
                 qa_ref, ka_ref, va_ref, la_ref, ra_ref,
                 qe_ref, qo_ref, ke_ref, ko_ref, vt_ref, sga_ref, sgb_ref,
                 fcar_ref, *, tiles_per_batch):
    i = pl.program_id(0)

    @pl.when(i % tiles_per_batch == 0)
    def _():
        fcar_ref[...] = jnp.zeros_like(fcar_ref)

    for r0 in range(0, x_ref.shape[0], PROJ_SUB):
        _proj_rows(r0, x_ref, mod_ref, gpre_ref, w_ref, wa2_ref, ba_ref, bsm_ref, eqk_ref,
                   qa_ref, ka_ref, va_ref, la_ref, ra_ref,
                   qe_ref, qo_ref, ke_ref, ko_ref, vt_ref, sga_ref, sgb_ref, fcar_ref)


def _proj_rows(r0, x_ref, mod_ref, gpre_ref, w_ref, wa2_ref, ba_ref, bsm_ref, eqk_ref,
               qa_ref, ka_ref, va_ref, la_ref, ra_ref,
               qe_ref, qo_ref, ke_ref, ko_ref, vt_ref, sga_ref, sgb_ref, fcar_ref):
    tm = PROJ_SUB
    rs = pl.ds(r0, tm)
    cs = slice(r0, r0 + tm)
    x = x_ref[rs, :]
    sh = mod_ref[0, 3:4, :]
    sc = mod_ref[0, 4:5, :]
    hb = ((_rms(x) * gpre_ref[...]) * (1.0 + sc) + sh).astype(BF16)

    def proj(c0, cw):
        return jnp.dot(hb, w_ref[:, c0:c0 + cw], preferred_element_type=F32)

    qa_ref[rs, :] = proj(_C_QA, GLA_DK) * (GLA_HK ** -0.5)
    ka_ref[rs, :] = proj(_C_KA, GLA_DK)
    va_ref[rs, :] = proj(_C_VA, GLA_DV).astype(BF16)
    r = proj(_C_RA, GLA_DV)
    ra_ref[rs, :] = (r * jax.nn.sigmoid(r)).astype(BF16)

    zs = proj(_C_SMALL, LANES)
    xa = jnp.dot(zs.astype(BF16), wa2_ref[...], preferred_element_type=F32) + ba_ref[...]
    la_ref[rs, :] = _log_sigmoid(xa) * (1.0 / GLA_TAU)

    lane = lax.broadcasted_iota(jnp.int32, (tm, LANES), 1)
    row = lax.broadcasted_iota(jnp.int32, (tm, LANES), 0)
    in_f = (lane >= _F_LANE0) & (lane < _F_LANE0 + FOX_HEADS)
    f = jnp.where(in_f, _log_sigmoid(zs + bsm_ref[...]), 0.0)
    shift = 1
    while shift < tm:
        f = f + jnp.where(row >= shift, pltpu.roll(f, shift, 0), 0.0)
        shift *= 2
    f = f + fcar_ref[...]
    fcar_ref[...] = f[tm - 1:tm, :]

    f2 = f * LOG2E
    p0 = f2.astype(BF16).astype(F32)
    r1 = f2 - p0
    p1 = r1.astype(BF16).astype(F32)
    p2 = (r1 - p1).astype(BF16).astype(F32)
    fc = p0 + pltpu.roll(p1, FOX_HEADS, 1) + pltpu.roll(p2, 2 * FOX_HEADS, 1)
    fc = jnp.where(lane == 0, 1.0, fc)
    aug = jnp.dot(fc.astype(BF16), eqk_ref[...], preferred_element_type=F32)
    augq = aug[:, :FOX_W]
    augk = aug[:, FOX_W:]

    lane_w = lax.broadcasted_iota(jnp.int32, (tm, FOX_W), 1)
    low_half = (lane_w & (LANES - 1)) < FOX_HD
    zq = proj(_C_QB, FOX_W) * (FOX_HD ** -0.5 * LOG2E)
    q_even = jnp.where(low_half, zq, augq)
    q_odd = jnp.where(low_half, augq, zq)
    for j in range(FOX_PAIRS):
        qe_ref[0, j, 0, :, cs] = q_even[:, j * LANES:(j + 1) * LANES].T.astype(BF16)
        qo_ref[0, j, 0, :, cs] = q_odd[:, j * LANES:(j + 1) * LANES].T.astype(BF16)
    zk = proj(_C_KB, FOX_W)
    ke_ref[rs, :] = jnp.where(low_half, zk, augk).astype(BF16)
    ko_ref[rs, :] = jnp.where(low_half, augk, zk).astype(BF16)
    zv = proj(_C_VB, FOX_W)
    for j in range(FOX_PAIRS):
        vt_ref[0, j, 0, :, cs] = zv[:, j * LANES:(j + 1) * LANES].T.astype(BF16)
    sga_ref[rs, :] = jax.nn.sigmoid(proj(_C_GA, D_MODEL)).astype(BF16)
    sgb_ref[rs, :] = jax.nn.sigmoid(proj(_C_GB, D_MODEL)).astype(BF16)


def _proj(x2d, mods, g_pre, w_all, w_a2p, b_a, b_small, e_qk, *, batch, seq):
    n = x2d.shape[0]
    tm = PROJ_TM
    tiles_per_batch = seq // tm
    row_spec = lambda w: pl.BlockSpec((tm, w), lambda i: (i, 0))
    transposed = jax.ShapeDtypeStruct((batch, FOX_PAIRS, tiles_per_batch, LANES, tm), BF16)
    out_shapes = [
        jax.ShapeDtypeStruct((n, GLA_DK), F32),
        jax.ShapeDtypeStruct((n, GLA_DK), F32),
        jax.ShapeDtypeStruct((n, GLA_DV), BF16),
        jax.ShapeDtypeStruct((n, GLA_DK), F32),
        jax.ShapeDtypeStruct((n, GLA_DV), BF16),
        transposed,
        transposed,
        jax.ShapeDtypeStruct((n, FOX_W), BF16),
        jax.ShapeDtypeStruct((n, FOX_W), BF16),
        transposed,
        jax.ShapeDtypeStruct((n, D_MODEL), BF16),
        jax.ShapeDtypeStruct((n, D_MODEL), BF16),
    ]
    return pl.pallas_call(
        functools.partial(_proj_kernel, tiles_per_batch=tiles_per_batch),
        out_shape=out_shapes,
        grid=(n // tm,),
        in_specs=[row_spec(D_MODEL),
                  pl.BlockSpec((1, N_MOD, D_MODEL), lambda i: (i // tiles_per_batch, 0, 0)),
                  _resident((1, D_MODEL)),
                  _resident((D_MODEL, _W_COLS)),
                  _resident((LANES, GLA_DK)),
                  _resident((1, GLA_DK)),
                  _resident((1, LANES)),
                  _resident((LANES, 2 * FOX_W))],
        out_specs=[row_spec(s.shape[1]) if len(s.shape) == 2 else
                   pl.BlockSpec((1, FOX_PAIRS, 1, LANES, tm),
                                lambda i: (i // tiles_per_batch, 0, i % tiles_per_batch, 0, 0))
                   for s in out_shapes],
        scratch_shapes=[pltpu.VMEM((1, LANES), F32)],
        compiler_params=pltpu.CompilerParams(dimension_semantics=("arbitrary",),
                                             vmem_limit_bytes=VMEM_LIMIT),
        name="proj",
    )(x2d, mods, g_pre, w_all, w_a2p, b_a, b_small, e_qk)


def _gla_tables():
    c = GLA_CHUNK
    t = np.arange(c)[:, None]
    u = np.arange(c)[None, :]
    masks = [(t == u)]
    for s in GLA_LEVELS:
        same = (t // (2 * s)) == (u // (2 * s))
        masks.append(same & (t % (2 * s) >= s) & (u % (2 * s) < s))
    ltri = (u <= t).astype(np.float32)
    mask = np.stack(masks, axis=0).astype(np.float32)
    return ltri, mask


def _dot_nt(a, b):
    return lax.dot_general(a, b, (((1,), (1,)), ((), ())), preferred_element_type=F32)


def _gla_pivots(b, b_rows_ref, s):
    c, dk = b.shape
    if s >= 8:
        return jnp.concatenate(
            [jnp.broadcast_to(b_rows_ref[pl.ds(g0 + s, 1), :], (2 * s, dk))
             for g0 in range(0, c, 2 * s)], axis=0)
    b3 = b.reshape(c // 8, 8, dk)
    sub = lax.broadcasted_iota(jnp.int32, b3.shape, 1)
    offset = s - (sub & (2 * s - 1))
    piv = b3
    for d in range(-(s - 1), s + 1):
        if d != 0:
            piv = jnp.where(offset == d, pltpu.roll(b3, (-d) % 8, 1), piv)
    return piv.reshape(c, dk)


def _gla_kernel(q_ref, k_ref, v_ref, g_ref, r_ref, gg_ref, ltri_ref, mask_ref, o_ref, s_ref, b_ref):
    c = GLA_CHUNK

    @pl.when(pl.program_id(2) == 0)
    def _():
        s_ref[...] = jnp.zeros_like(s_ref)

    n_chunks = q_ref.shape[0] // c
    states = [s_ref[hd] for hd in range(GLA_HPS)]
    for ci in range(n_chunks):
        sl = pl.ds(ci * c, c)
        for hd in range(GLA_HPS):
            ks = slice(hd * GLA_HK, (hd + 1) * GLA_HK)
            vs = slice(hd * GLA_HV, (hd + 1) * GLA_HV)
            slot = ci * GLA_HPS + hd
            q = q_ref[sl, ks]
            k = k_ref[sl, ks]
            g = g_ref[sl, ks]
            v = v_ref[sl, vs]
            g_hi = g.astype(BF16)
            g_lo = (g - g_hi.astype(F32)).astype(BF16)
            x2 = jnp.dot(ltri_ref[...], jnp.concatenate([g_hi, g_lo], axis=1),
                         preferred_element_type=F32)
            b = (x2[:, :GLA_HK] + x2[:, GLA_HK:]) * LOG2E
            b_ref[slot] = b
            eb = jnp.exp2(b)
            ebl = jnp.exp2(b_ref[slot, pl.ds(c - 1, 1), :] - b)

            a = mask_ref[0] * _dot_nt(q.astype(BF16), k.astype(BF16))
            for li, s in enumerate(GLA_LEVELS):
                f = jnp.exp2(-jnp.abs(b - _gla_pivots(b, b_ref.at[slot], s)))
                a = a + mask_ref[li + 1] * _dot_nt((q * f).astype(BF16), (k * f).astype(BF16))

            lhs = jnp.concatenate([(q * eb).astype(BF16), a.astype(BF16)], axis=1)
            rhs = jnp.concatenate([states[hd].astype(BF16), v], axis=0)
            o = jnp.dot(lhs, rhs, preferred_element_type=F32)

            upd = jnp.dot((k * ebl).T.astype(BF16), v, preferred_element_type=F32)
            decay = eb.T[:, c - 1:c]
            states[hd] = decay * states[hd] + upd

            on = _rms(o) * gg_ref[hd]
            o_ref[sl, vs] = (on * r_ref[sl, vs].astype(F32)).astype(BF16)
    for hd in range(GLA_HPS):
        s_ref[hd] = states[hd]


def _gla(qa, ka, va, la, ra, g_gla, ltri, mask, *, batch, seq):
    n = qa.shape[0]
    tiles = seq // GLA_TT
    rows = lambda b, h, t: b * tiles + t
    return pl.pallas_call(
        _gla_kernel,
        out_shape=jax.ShapeDtypeStruct((n, GLA_DV), BF16),
        grid=(batch, GLA_HEADS // GLA_HPS, tiles),
        in_specs=[pl.BlockSpec((GLA_TT, GLA_HPS * GLA_HK), lambda b, h, t: (rows(b, h, t), h)),
                  pl.BlockSpec((GLA_TT, GLA_HPS * GLA_HK), lambda b, h, t: (rows(b, h, t), h)),
                  pl.BlockSpec((GLA_TT, GLA_HPS * GLA_HV), lambda b, h, t: (rows(b, h, t), h)),
                  pl.BlockSpec((GLA_TT, GLA_HPS * GLA_HK), lambda b, h, t: (rows(b, h, t), h)),
                  pl.BlockSpec((GLA_TT, GLA_HPS * GLA_HV), lambda b, h, t: (rows(b, h, t), h)),
                  pl.BlockSpec((GLA_HPS, 1, GLA_HV), lambda b, h, t: (h, 0, 0)),
                  _resident(ltri.shape),
                  _resident(mask.shape)],
        out_specs=pl.BlockSpec((GLA_TT, GLA_HPS * GLA_HV), lambda b, h, t: (rows(b, h, t), h)),
        scratch_shapes=[pltpu.VMEM((GLA_HPS, GLA_HK, GLA_HV), F32),
                        pltpu.VMEM((GLA_HPS * GLA_TT // GLA_CHUNK, GLA_CHUNK, GLA_HK), F32)],
        compiler_params=pltpu.CompilerParams(
            dimension_semantics=("arbitrary", "arbitrary", "arbitrary"),
            vmem_limit_bytes=VMEM_LIMIT),
        name="gla",
    )(qa, ka, va, la, ra, g_gla, ltri, mask)


def _fox_kernel(qe_ref, qo_ref, qen_ref, qon_ref, ke_ref, ko_ref, vt_ref, o_ref,
                m_ref, acc_ref, s0_ref, sa_ref, sb_ref):
    i = pl.program_id(2)
    n_vt = qe_ref.shape[2]
    bq = n_vt * qe_ref.shape[-1]
    bk = bq
    q_cur = (qe_ref, qo_ref)
    q_next = (qen_ref, qon_ref)
    k_refs = (ke_ref, ko_ref)

    m_ref[...] = jnp.full_like(m_ref, -jnp.inf)
    acc_ref[...] = jnp.zeros_like(acc_ref)

    def logits(dst_ref, t, q_refs=q_cur):
        ks = pl.ds(pl.multiple_of(t * bk, bk), bk)
        for hd in range(2):
            qt = jnp.concatenate([q_refs[hd][0, 0, c] for c in range(n_vt)], axis=1)
            dst_ref[hd] = jnp.dot(k_refs[hd][ks, :], qt, preferred_element_type=F32)

    def consume(src_ref, t, masked):
        vt = jnp.concatenate([vt_ref[0, 0, t * n_vt + c] for c in range(n_vt)], axis=1)
        ones = jnp.ones((_FOX_SUM_ROWS, bk), BF16)
        for hd in range(2):
            vaug = jnp.concatenate([vt[hd * FOX_HD:(hd + 1) * FOX_HD], ones], axis=0)
            for c0 in range(0, bq, _FOX_QSUB):
                cs = slice(c0, c0 + _FOX_QSUB)
                s = src_ref[hd, :, cs]
                if masked:
                    k_i = lax.broadcasted_iota(jnp.int32, s.shape, 0)
                    q_i = lax.broadcasted_iota(jnp.int32, s.shape, 1) + c0
                    s = jnp.where(k_i <= q_i, s, -jnp.inf)
                m_old = m_ref[hd, :, cs]
                m_new = jnp.maximum(m_old, jnp.max(s, axis=0, keepdims=True))
                p = jnp.exp2(s - m_new).astype(BF16)
                acc_ref[hd, :, cs] = jnp.exp2(m_old - m_new) * acc_ref[hd, :, cs] + jnp.dot(
                    vaug, p, preferred_element_type=F32)
                m_ref[hd, :, cs] = m_new

    def pair(u, carry=None):
        logits(sb_ref, 2 * u + 2)
        consume(sa_ref, 2 * u + 1, False)
        logits(sa_ref, 2 * u + 3)
        consume(sb_ref, 2 * u + 2, False)
        return carry

    def two_pairs(w, carry=None):
        pair(2 * w)
        pair(2 * w + 1)
        return carry

    def four_pairs(z, carry):
        two_pairs(2 * z)
        two_pairs(2 * z + 1)
        return carry

    @pl.when(i == 0)
    def _():
        logits(s0_ref, 0)
        consume(s0_ref, 0, True)
        logits(s0_ref, 0, q_next)

    n_pairs = (i - 1) // 2

    @pl.when(i > 0)
    def _():
        logits(sa_ref, 1)
        consume(s0_ref, 0, False)
        lax.fori_loop(0, n_pairs // 4, four_pairs, 0)

    @pl.when((i > 0) & (n_pairs % 4 >= 2))
    def _():
        two_pairs(2 * (n_pairs // 4))

    @pl.when((i > 0) & (n_pairs % 2 == 1))
    def _():
        pair(n_pairs - 1)

    @pl.when(i % 2 == 1)
    def _():
        logits(s0_ref, 0, q_next)
        consume(sa_ref, i, True)

    @pl.when((i > 0) & (i % 2 == 0))
    def _():
        logits(sb_ref, i)
        consume(sa_ref, i - 1, False)
        logits(s0_ref, 0, q_next)
        consume(sb_ref, i, True)

    oe = acc_ref[0]
    oo = acc_ref[1]
    ot = jnp.concatenate([oe[:FOX_HD] / oe[FOX_HD:FOX_HD + 1],
                          oo[:FOX_HD] / oo[FOX_HD:FOX_HD + 1]], axis=0)
    o_ref[...] = ot.T.astype(BF16)


def _fox(qe, qo, ke, ko, vt, *, batch, seq):
    n = batch * seq
    bq = FOX_BQ
    tiles = seq // bq
    q_blk = (1, 1, bq // vt.shape[-1]) + vt.shape[3:]
    q_spec = pl.BlockSpec(q_blk, lambda b, j, i: (b, j, i, 0, 0))
    q_next_spec = pl.BlockSpec(q_blk, lambda b, j, i: (b, j, jnp.minimum(i + 1, tiles - 1), 0, 0))
    kv_spec = pl.BlockSpec((seq, LANES), lambda b, j, i: (b, j))
    logit_buf = pltpu.VMEM((2, bq, bq), F32)
    return pl.pallas_call(
        _fox_kernel,
        out_shape=jax.ShapeDtypeStruct((n, FOX_W), BF16),
        grid=(batch, FOX_PAIRS, tiles),
        in_specs=[q_spec, q_spec, q_next_spec, q_next_spec, kv_spec, kv_spec,
                  pl.BlockSpec((1, 1) + vt.shape[2:], lambda b, j, i: (b, j, 0, 0, 0))],
        out_specs=pl.BlockSpec((bq, LANES), lambda b, j, i: (b * tiles + i, j)),
        scratch_shapes=[pltpu.VMEM((2, 1, bq), F32),
                        pltpu.VMEM((2, FOX_HD + _FOX_SUM_ROWS, bq), F32),
                        logit_buf, logit_buf, logit_buf],
        compiler_params=pltpu.CompilerParams(
            dimension_semantics=("arbitrary", "arbitrary", "arbitrary"),
            vmem_limit_bytes=VMEM_LIMIT),
        name="fox",
    )(qe, qo, qe, qo, ke, ko, vt)


def _merge_kernel(x_ref, mod_ref, gpost_ref, oa_ref, ob_ref, sga_ref, sgb_ref,
                  wpa_ref, wpb_ref, wout_ref, o_ref):
    gt = mod_ref[0, 5:6, :]
    for r0 in range(0, x_ref.shape[0], MERGE_SUB):
        rs = pl.ds(r0, MERGE_SUB)
        pa = jnp.dot(oa_ref[rs, :], wpa_ref[...].astype(BF16), preferred_element_type=F32)
        pb = jnp.dot(ob_ref[rs, :], wpb_ref[...].astype(BF16), preferred_element_type=F32)
        merged = sga_ref[rs, :].astype(F32) * pa + sgb_ref[rs, :].astype(F32) * pb
        y = jnp.dot(merged.astype(BF16), wout_ref[...].astype(BF16), preferred_element_type=F32)
        o_ref[rs, :] = x_ref[rs, :] + gt * (_rms(y) * gpost_ref[...])


def _merge(x2d, mods, g_post, oa, ob, sga, sgb, w_pa, w_pb, w_out, *, seq):
    n = x2d.shape[0]
    tm = MERGE_TM
    tiles_per_batch = seq // tm
    row_spec = pl.BlockSpec((tm, D_MODEL), lambda i: (i, 0))
    return pl.pallas_call(
        _merge_kernel,
        out_shape=jax.ShapeDtypeStruct((n, D_MODEL), F32),
        grid=(n // tm,),
        in_specs=[row_spec,
                  pl.BlockSpec((1, N_MOD, D_MODEL), lambda i: (i // tiles_per_batch, 0, 0)),
                  _resident((1, D_MODEL)),
                  row_spec, row_spec, row_spec, row_spec,
                  _resident((D_MODEL, D_MODEL)),
                  _resident((D_MODEL, D_MODEL)),
                  _resident((D_MODEL, D_MODEL))],
        out_specs=row_spec,
        compiler_params=pltpu.CompilerParams(dimension_semantics=("arbitrary",),
                                             vmem_limit_bytes=VMEM_LIMIT),
        name="merge",
    )(x2d, mods, g_post, oa, ob, sga, sgb, w_pa, w_pb, w_out)


def _fox_bias_placement():
    e = np.zeros((LANES, 2 * FOX_W), np.float32)
    for h in range(FOX_HEADS):
        base = LANES * (h // 2) + (FOX_HD if h % 2 == 0 else 0)
        for p in range(3):
            src = _F_LANE0 + FOX_HEADS * p + h
            e[src, base + p] = 1.0
            e[0, base + 3 + p] = 1.0
            e[0, FOX_W + base + p] = 1.0
            e[src, FOX_W + base + 3 + p] = -1.0
    return e


def kernel(x, c, w_ada, b_ada, g_pre, g_post, w_gu1, w_dn1, w_gu2, w_dn2,
           w_in, w_a2, b_a, b_f, g_gla, w_pa, w_pb, w_out):
    batch, seq, d = x.shape
    n = batch * seq
    depth = w_ada.shape[0]
    ltri_np, mask_np = _gla_tables()
    ltri = jnp.asarray(ltri_np, BF16)
    mask = jnp.asarray(mask_np, F32)
    e_qk = jnp.asarray(_fox_bias_placement(), BF16)

    x2d = x.reshape(n, d)
    c_pad = jnp.pad(c, ((0, 16 - batch), (0, 0)))
    for l in range(depth):
        mods = _adaln(c_pad, w_ada[l], b_ada[l][None, :])[:batch].reshape(batch, N_MOD, d)

        x2d = _ffn(x2d, mods, g_pre[l, 0][None, :], g_post[l, 0][None, :],
                   w_gu1[l], w_dn1[l], mod0=0, seq=seq)

        w_all = _regroup(jnp.swapaxes(w_in[l], 0, 1))
        w_a2p = jnp.pad(w_a2[l], ((0, LANES - GLA_RANK), (0, 0))).astype(BF16)
        b_small = jnp.pad(b_f[l], (_F_LANE0, LANES - _F_LANE0 - FOX_HEADS))[None, :]

        (qa, ka, va, la, ra, qe, qo, ke, ko, vt, sga, sgb) = _proj(
            x2d, mods, g_pre[l, 1][None, :], w_all, w_a2p, b_a[l][None, :], b_small, e_qk,
            batch=batch, seq=seq)

        oa = _gla(qa, ka, va, la, ra, g_gla[l].reshape(GLA_HEADS, 1, GLA_HV), ltri, mask,
                  batch=batch, seq=seq)
        ob = _fox(qe, qo, ke, ko, vt, batch=batch, seq=seq)

        x2d = _merge(x2d, mods, g_post[l, 1][None, :], oa, ob, sga, sgb,
                     w_pa[l], w_pb[l], w_out[l], seq=seq)

        x2d = _ffn(x2d, mods, g_pre[l, 2][None, :], g_post[l, 2][None, :],
                   w_gu2[l], w_dn2[l], mod0=6, seq=seq)
    return x2d.reshape(batch, seq, d)
```
